```python
import jax, jax.numpy as jnp
from jax import lax
import numpy as np


D_MODEL = 1024
BATCH = 16
SEQ = 2048
DEPTH = 2

RET_HEADS = 4
RET_DK = 128
RET_DV = 256
RET_CHUNK = 128
GLA_HEADS = 4
GLA_DK = 128
GLA_DV = 256
GLA_RANK = 16
GLA_TAU = 16.0
GLA_LOG_GATE_MIN = -1.0
GLA_CHUNK = 64
SWA_Q_HEADS = 16
SWA_KV_HEADS = 4
SWA_HEAD_DIM = 64
SWA_WINDOW = 128
N_BRANCH = 3
BRANCH_WIDTH = RET_HEADS * RET_DV
D_FF = -(-8 * D_MODEL // (3 * 256)) * 256
NORM_EPS = 1e-6
GN_EPS = 1e-5

IN_SPLITS = (
    RET_HEADS * RET_DK, RET_HEADS * RET_DK, RET_HEADS * RET_DV, RET_HEADS * RET_DV,
    GLA_HEADS * GLA_DK, GLA_HEADS * GLA_DK, GLA_HEADS * GLA_DV, GLA_HEADS * GLA_DV, GLA_RANK,
    SWA_Q_HEADS * SWA_HEAD_DIM, SWA_KV_HEADS * SWA_HEAD_DIM, SWA_KV_HEADS * SWA_HEAD_DIM,
    N_BRANCH * D_MODEL,
)
D_IN = sum(IN_SPLITS)

kernel_name = "hybrid_retnet_gla_swa_gated_block"


def _split_points():
    return [int(o) for o in np.cumsum(IN_SPLITS)[:-1]]


def rmsnorm(x, w):
    xf = x.astype(jnp.float32)
    y = xf * lax.rsqrt(jnp.mean(xf * xf, axis=-1, keepdims=True) + NORM_EPS)
    return (y * w.astype(jnp.float32)).astype(x.dtype)


def head_groupnorm(o, w, b):
    B, S, H, dv = o.shape
    of = o.astype(jnp.float32)
    mu = jnp.mean(of, axis=-1, keepdims=True)
    var = jnp.mean(jnp.square(of - mu), axis=-1, keepdims=True)
    y = ((of - mu) * lax.rsqrt(var + GN_EPS)).reshape(B, S, H * dv)
    return y * w.astype(jnp.float32) + b.astype(jnp.float32)


def head_rmsnorm(o, w):
    B, S, H, dv = o.shape
    of = o.astype(jnp.float32)
    y = of * lax.rsqrt(jnp.mean(of * of, axis=-1, keepdims=True) + NORM_EPS)
    return y.reshape(B, S, H * dv) * w.astype(jnp.float32)


def retention(q, k, v):
    B, S, H, dk = q.shape
    dv = v.shape[-1]
    C = RET_CHUNK
    N = S // C
    log_g = jnp.log1p(-jnp.exp2(-5.0 - jnp.arange(H, dtype=jnp.float32)))
    q = (q * dk ** -0.5).reshape(B, N, C, H, dk)
    k = k.reshape(B, N, C, H, dk)
    v = v.reshape(B, N, C, H, dv)
    pos = jnp.arange(C, dtype=jnp.float32)
    rel = pos[:, None] - pos[None, :]
    decay = jnp.exp(jnp.where((rel >= 0)[None], rel[None] * log_g[:, None, None], -jnp.inf))
    scores = jnp.einsum('bnihd,bnjhd->bnhij', q, k) * decay
    intra = jnp.einsum('bnhij,bnjhe->bnihe', scores, v)
    zeta = jnp.exp((C - 1 - pos)[None, :] * log_g[:, None])
    kv = jnp.einsum('bnjhd,bnjhe,hj->nbhde', k, v, zeta)
    gamma_c = jnp.exp(C * log_g)[None, :, None, None]

    def step(R, kv_n):
        return gamma_c * R + kv_n, R

    _, states = lax.scan(step, jnp.zeros(kv.shape[1:], kv.dtype), kv)
    xi = jnp.exp((pos + 1.0)[:, None] * log_g[None, :])
    inter = jnp.einsum('bnihd,nbhde->bnihe', q * xi[:, :, None], states)
    return (intra + inter).reshape(B, S, H, dv)


def gla(q, k, v, log_a):
    B, S, H, dk = q.shape
    dv = v.shape[-1]
    C = GLA_CHUNK
    N = S // C
    q = (q * dk ** -0.5).reshape(B, N, C, H, dk)
    k = k.reshape(B, N, C, H, dk)
    v = v.reshape(B, N, C, H, dv)
    b = jnp.cumsum(log_a.reshape(B, N, C, H, dk), axis=2)
    b_last = b[:, :, -1:]
    q_dec = q * jnp.exp(b)
    k_inv = k * jnp.exp(-b)
    causal = jnp.tril(jnp.ones((C, C), dtype=bool))
    A = jnp.where(causal, jnp.einsum('bnihd,bnjhd->bnhij', q_dec, k_inv), 0.0)
    intra = jnp.einsum('bnhij,bnjhe->bnihe', A, v)
    kv = jnp.einsum('bnjhd,bnjhe->nbhde', k * jnp.exp(b_last - b), v)
    chunk_decay = jnp.moveaxis(jnp.exp(b_last[:, :, 0]), 1, 0)

    def step(St, inp):
        kv_n, a_n = inp
        return a_n[..., None] * St + kv_n, St

    _, states = lax.scan(step, jnp.zeros(kv.shape[1:], kv.dtype), (kv, chunk_decay))
    inter = jnp.einsum('bnihd,nbhde->bnihe', q_dec, states)
    return (intra + inter).reshape(B, S, H, dv)


def swa_sinks(q, k, v, sinks):
    B, S, Hq, dh = q.shape
    Hkv = k.shape[2]
    G = Hq // Hkv
    W = SWA_WINDOW
    N = S // W
    slopes = jnp.exp2(-8.0 * jnp.arange(1, Hq + 1, dtype=jnp.float32) / Hq).reshape(Hkv, G)
    qb = q.reshape(B, N, W, Hkv, G, dh)
    kb = k.reshape(B, N, W, Hkv, dh)
    vb = v.reshape(B, N, W, Hkv, dh)

    def with_prev(t):
        prev = jnp.pad(t[:, :-1], ((0, 0), (1, 0), (0, 0), (0, 0), (0, 0)))
        return jnp.concatenate([prev, t], axis=2)

    kk, vv = with_prev(kb), with_prev(vb)
    scores = jnp.einsum('bnqhgd,bnkhd->bnhgqk', qb, kk).astype(jnp.float32) * dh ** -0.5
    qpos = W + jnp.arange(W)
    kpos = jnp.arange(2 * W)
    dist = qpos[:, None] - kpos[None, :]
    blk = jnp.arange(N)
    valid = ((dist >= 0) & (dist < W))[None] & ((blk[:, None, None] > 0) | (kpos[None, None, :] >= W))
    logits = scores - slopes[:, :, None, None] * dist.astype(jnp.float32)
    logits = jnp.where(valid[None, :, None, None], logits, -jnp.inf)
    sink = sinks.astype(jnp.float32).reshape(Hkv, G)[:, :, None, None]
    m = jnp.maximum(jnp.max(logits, axis=-1, keepdims=True), sink)
    p = jnp.exp(logits - m)
    probs = p / (jnp.sum(p, axis=-1, keepdims=True) + jnp.exp(sink - m))
    out = jnp.einsum('bnhgqk,bnkhd->bnqhgd', probs.astype(v.dtype), vv)
    return out.reshape(B, S, Hq * dh)


def setup_inputs(seed: int = 0) -> dict:
    key = jax.random.key(seed)
    ks = jax.random.split(key, 17)
    f32 = jnp.float32

    def nrm(k, shape, scale):
        return jax.random.normal(k, shape, f32) * scale

    def gain(k, n):
        return 1.0 + nrm(k, (DEPTH, n), 0.05)

    return {
        'x': nrm(ks[0], (BATCH, SEQ, D_MODEL), 1.0),
        'norm_mix_pre': gain(ks[1], D_MODEL),
        'norm_mix_post': gain(ks[2], D_MODEL),
        'w_in': nrm(ks[3], (DEPTH, D_MODEL, D_IN), D_MODEL ** -0.5),
        'ret_norm_w': gain(ks[4], RET_HEADS * RET_DV),
        'ret_norm_b': nrm(ks[5], (DEPTH, RET_HEADS * RET_DV), 0.02),
        'gla_w_alpha2': nrm(ks[6], (DEPTH, GLA_RANK, GLA_HEADS * GLA_DK), GLA_RANK ** -0.5),
        'gla_b_alpha': nrm(ks[7], (DEPTH, GLA_HEADS * GLA_DK), 0.1),
        'gla_norm_w': gain(ks[8], GLA_HEADS * GLA_DV),
        'attn_sinks': nrm(ks[9], (DEPTH, SWA_Q_HEADS), 0.5),
        'w_branch': nrm(ks[10], (DEPTH, N_BRANCH, BRANCH_WIDTH, D_MODEL), BRANCH_WIDTH ** -0.5),
        'w_out': nrm(ks[11], (DEPTH, D_MODEL, D_MODEL), D_MODEL ** -0.5),
        'norm_ffn_pre': gain(ks[12], D_MODEL),
        'norm_ffn_post': gain(ks[13], D_MODEL),
        'ffn_w_gate': nrm(ks[14], (DEPTH, D_MODEL, D_FF), D_MODEL ** -0.5),
        'ffn_w_up': nrm(ks[15], (DEPTH, D_MODEL, D_FF), D_MODEL ** -0.5),
        'ffn_w_down': nrm(ks[16], (DEPTH, D_FF, D_MODEL), D_FF ** -0.5),
    }


def reference(x, norm_mix_pre, norm_mix_post, w_in, ret_norm_w, ret_norm_b, gla_w_alpha2,
              gla_b_alpha, gla_norm_w, attn_sinks, w_branch, w_out, norm_ffn_pre, norm_ffn_post,
              ffn_w_gate, ffn_w_up, ffn_w_down):
    B, S, _ = x.shape
    split_points = _split_points()
    for l in range(DEPTH):
        h = rmsnorm(x, norm_mix_pre[l])
        proj = h @ w_in[l]
        (rq, rk, rv, rg, gq, gk, gv, gg, ga, sq, sk, sv, gate_logits) = jnp.split(proj, split_points, axis=-1)

        o_ret = retention(rq.reshape(B, S, RET_HEADS, RET_DK), rk.reshape(B, S, RET_HEADS, RET_DK),
                          rv.reshape(B, S, RET_HEADS, RET_DV))
        o_ret = (head_groupnorm(o_ret, ret_norm_w[l], ret_norm_b[l]) * jax.nn.silu(rg)).astype(x.dtype)

        z = (ga @ gla_w_alpha2[l] + gla_b_alpha[l]).astype(jnp.float32)
        log_a = jnp.maximum(jax.nn.log_sigmoid(z) / GLA_TAU, GLA_LOG_GATE_MIN)
        o_gla = gla(gq.reshape(B, S, GLA_HEADS, GLA_DK), gk.reshape(B, S, GLA_HEADS, GLA_DK),
                    gv.reshape(B, S, GLA_HEADS, GLA_DV), log_a.reshape(B, S, GLA_HEADS, GLA_DK))
        o_gla = (head_rmsnorm(o_gla, gla_norm_w[l]) * jax.nn.silu(gg)).astype(x.dtype)

        o_swa = swa_sinks(sq.reshape(B, S, SWA_Q_HEADS, SWA_HEAD_DIM),
                          sk.reshape(B, S, SWA_KV_HEADS, SWA_HEAD_DIM),
                          sv.reshape(B, S, SWA_KV_HEADS, SWA_HEAD_DIM), attn_sinks[l]).astype(x.dtype)

        branches = jnp.stack([o_ret, o_gla, o_swa], axis=2)
        projected = jnp.einsum('bsnc,ncd->bsnd', branches, w_branch[l])
        gates = jax.nn.sigmoid(gate_logits.reshape(B, S, N_BRANCH, D_MODEL))
        merged = jnp.sum(gates * projected, axis=2)
        x = x + rmsnorm(merged @ w_out[l], norm_mix_post[l])

        h = rmsnorm(x, norm_ffn_pre[l])
        f = (jax.nn.silu(h @ ffn_w_gate[l]) * (h @ ffn_w_up[l])) @ ffn_w_down[l]
        x = x + rmsnorm(f, norm_ffn_post[l])
    return x
```

```python
import functools
import math

import jax
import jax.numpy as jnp
from jax import lax
from jax.experimental import pallas as pl
from jax.experimental.pallas import tpu as pltpu

F32 = jnp.float32
BF16 = jnp.bfloat16

D_MODEL = 1024
DEPTH = 2
RET_HEADS, RET_DK, RET_DV, RET_CHUNK = 4, 128, 256, 128
GLA_HEADS, GLA_DK, GLA_DV, GLA_RANK, GLA_CHUNK = 4, 128, 256, 16, 64
GLA_TAU, GLA_LOG_GATE_MIN = 16.0, -1.0
SWA_Q_HEADS, SWA_KV_HEADS, SWA_HEAD_DIM, SWA_WINDOW = 16, 4, 64, 128
N_BRANCH = 3
D_FF = 2816
NORM_EPS = 1e-6
GN_EPS = 1e-5

LANES = 128
VMEM_LIMIT_BYTES = 56 * 1024 * 1024

_O_RET, _O_GLA, _O_GA, _O_SWA, _O_GATE, _O_END = 0, 3072, 6144, 6160, 7696, 10768
P_GATE = 0
P_RQ, P_RK, P_RV, P_RG = 3072, 3584, 4096, 5120
P_GQ, P_GK, P_GV, P_GG = 6144, 6656, 7168, 8192
P_SQ, P_SK, P_SV = 9216, 10240, 10496
P_TOTAL = 10752


def _cparams(*sem):
    return pltpu.CompilerParams(dimension_semantics=sem, vmem_limit_bytes=VMEM_LIMIT_BYTES)


def _rms(x, w, eps):
    return x * lax.rsqrt(jnp.mean(x * x, axis=-1, keepdims=True) + eps) * w


def _dot(a, b):
    return jnp.dot(a, b, preferred_element_type=F32)


def _dot_nt(a, b):
    return lax.dot_general(a, b, (((1,), (1,)), ((), ())), preferred_element_type=F32)


def _inproj_kernel(x_ref, g_ref, w_ref, wga_ref, o_ref, oga_ref, h_ref):
    @pl.when(pl.program_id(1) == 0)
    def _():
        h = _rms(x_ref[...], g_ref[...], NORM_EPS).astype(BF16)
        h_ref[...] = h
        oga_ref[...] = _dot(h, wga_ref[...]).astype(BF16)

    o_ref[...] = _dot(h_ref[...], w_ref[...]).astype(BF16)


def _inproj(x2, gain, w_main, w_ga, tm, tn):
    m, d = x2.shape
    n = w_main.shape[1]
    return pl.pallas_call(
        _inproj_kernel,
        grid=(m // tm, n // tn),
        in_specs=[
            pl.BlockSpec((tm, d), lambda i, j: (i, 0)),
            pl.BlockSpec((1, d), lambda i, j: (0, 0)),
            pl.BlockSpec((d, tn), lambda i, j: (0, j)),
            pl.BlockSpec((d, LANES), lambda i, j: (0, 0)),
        ],
        out_specs=[
            pl.BlockSpec((tm, tn), lambda i, j: (i, j)),
            pl.BlockSpec((tm, LANES), lambda i, j: (i, 0)),
        ],
        out_shape=[jax.ShapeDtypeStruct((m, n), BF16), jax.ShapeDtypeStruct((m, LANES), BF16)],
        scratch_shapes=[pltpu.VMEM((tm, d), BF16)],
        compiler_params=_cparams("arbitrary", "arbitrary"),
        name="inproj",
    )(x2, gain, w_main, w_ga)


def _ret_kernel(lg_ref, q_ref, k_ref, v_ref, g_ref, nw_ref, nb_ref, o_ref, *, n_chunks):
    c = RET_CHUNK
    lg = lg_ref[...]
    row = lax.broadcasted_iota(jnp.int32, (c, c), 0)
    col = lax.broadcasted_iota(jnp.int32, (c, c), 1)
    rel = (row - col).astype(F32)
    scale = RET_DK ** -0.5
    decay = jnp.where(row >= col, jnp.exp(rel * lg), 0.0) * scale
    rowf = row.astype(F32)
    xi = jnp.exp((rowf + 1.0) * lg) * scale
    zeta = jnp.exp((c - 1.0 - rowf) * lg)
    gamma_c = jnp.exp(c * lg)
    gamma_c = jnp.concatenate([gamma_c, gamma_c], axis=1)
    nw = nw_ref[...]
    nb = nb_ref[...]

    def body(i, state):
        r0 = pl.multiple_of(i * c, c)
        qc = q_ref[pl.ds(r0, c), :]
        kc = k_ref[pl.ds(r0, c), :]
        vc = v_ref[pl.ds(r0, c), :]
        s = _dot_nt(qc, kc) * decay
        qx = qc.astype(F32) * xi
        lhs = jnp.concatenate([s.astype(BF16), qx.astype(BF16)], axis=1)
        rhs = jnp.concatenate([vc, state.astype(BF16)], axis=0)
        o = _dot(lhs, rhs)
        kz = (kc.astype(F32) * zeta).T.astype(BF16)
        state = gamma_c * state + _dot(kz, vc)
        mu = jnp.mean(o, axis=-1, keepdims=True)
        d = o - mu
        var = jnp.mean(d * d, axis=-1, keepdims=True)
        y = d * lax.rsqrt(var + GN_EPS) * nw + nb
        gc = g_ref[pl.ds(r0, c), :].astype(F32)
        o_ref[pl.ds(r0, c), :] = (y * (gc * jax.nn.sigmoid(gc))).astype(BF16)
        return state

    lax.fori_loop(0, n_chunks, body, jnp.zeros((RET_DK, RET_DV), F32))


def _retention(proj, log_g, norm_w, norm_b, batch, seq):
    m = proj.shape[0]
    h = RET_HEADS
    kern = functools.partial(_ret_kernel, n_chunks=seq // RET_CHUNK)
    return pl.pallas_call(
        kern,
        grid=(batch, h),
        in_specs=[
            pl.BlockSpec((1, LANES), lambda b, hh: (0, hh)),
            pl.BlockSpec((seq, RET_DK), lambda b, hh: (b, P_RQ // RET_DK + hh)),
            pl.BlockSpec((seq, RET_DK), lambda b, hh: (b, P_RK // RET_DK + hh)),
            pl.BlockSpec((seq, RET_DV), lambda b, hh: (b, P_RV // RET_DV + hh)),
            pl.BlockSpec((seq, RET_DV), lambda b, hh: (b, P_RG // RET_DV + hh)),
            pl.BlockSpec((1, RET_DV), lambda b, hh: (0, hh)),
            pl.BlockSpec((1, RET_DV), lambda b, hh: (0, hh)),
        ],
        out_specs=pl.BlockSpec((seq, RET_DV), lambda b, hh: (b, hh)),
        out_shape=jax.ShapeDtypeStruct((m, h * RET_DV), BF16),
        compiler_params=_cparams("arbitrary", "arbitrary"),
        name="retention",
    )(log_g, proj, proj, proj, proj, norm_w, norm_b)


def _gla_kernel(ga_ref, w2_ref, ba_ref, q_ref, k_ref, v_ref, g_ref, nw_ref, o_ref, la_ref, *, n_pairs):
    c = GLA_CHUNK
    c2 = 2 * c
    z = _dot(ga_ref[...], w2_ref[...]) + ba_ref[...]
    log_sig = jnp.minimum(z, 0.0) - jnp.log1p(jnp.exp(-jnp.abs(z)))
    la_ref[...] = jnp.maximum(log_sig / GLA_TAU, GLA_LOG_GATE_MIN)

    row = lax.broadcasted_iota(jnp.int32, (c2, c2), 0)
    col = lax.broadcasted_iota(jnp.int32, (c2, c2), 1)
    same_chunk = (row >= c) == (col >= c)
    causal = (row >= col) & same_chunk
    tri = jnp.where(causal, 1.0, 0.0).astype(F32)
    scale = GLA_DK ** -0.5
    nw = nw_ref[...]

    def body(i, state):
        r0 = pl.multiple_of(i * c2, c2)
        la = la_ref[pl.ds(r0, c2), :]
        bcum = jnp.dot(tri, la, precision=lax.Precision.HIGHEST, preferred_element_type=F32)
        qf = q_ref[pl.ds(r0, c2), :].astype(F32) * scale
        kf = k_ref[pl.ds(r0, c2), :].astype(F32)
        vc = v_ref[pl.ds(r0, c2), :]
        q_dec = (qf * jnp.exp(bcum)).astype(BF16)
        k_inv = (kf * jnp.exp(-bcum)).astype(BF16)
        a = jnp.where(causal, _dot_nt(q_dec, k_inv), 0.0).astype(BF16)
        intra = _dot(a, vc)
        outs = []
        for half in range(2):
            lo, hi = half * c, (half + 1) * c
            b_last = bcum[hi - 1:hi, :]
            outs.append(intra[lo:hi] + _dot(q_dec[lo:hi], state.astype(BF16)))
            k_dec = kf[lo:hi] * jnp.exp(b_last - bcum[lo:hi])
            kv = _dot(k_dec.T.astype(BF16), vc[lo:hi])
            dcol = jnp.broadcast_to(jnp.exp(b_last), (GLA_DK, GLA_DK)).T
            dcol = jnp.concatenate([dcol, dcol], axis=1)
            state = dcol * state + kv
        o = jnp.concatenate(outs, axis=0)
        y = o * lax.rsqrt(jnp.mean(o * o, axis=-1, keepdims=True) + NORM_EPS) * nw
        gc = g_ref[pl.ds(r0, c2), :].astype(F32)
        o_ref[pl.ds(r0, c2), :] = (y * (gc * jax.nn.sigmoid(gc))).astype(BF16)
        return state

    lax.fori_loop(0, n_pairs, body, jnp.zeros((GLA_DK, GLA_DV), F32))


def _gla(proj, ga, w2p, b_alpha, norm_w, batch, seq):
    m = proj.shape[0]
    h = GLA_HEADS
    kern = functools.partial(_gla_kernel, n_pairs=seq // (2 * GLA_CHUNK))
    return pl.pallas_call(
        kern,
        grid=(batch, h),
        in_specs=[
            pl.BlockSpec((seq, LANES), lambda b, hh: (b, 0)),
            pl.BlockSpec((LANES, GLA_DK), lambda b, hh: (0, hh)),
            pl.BlockSpec((1, GLA_DK), lambda b, hh: (0, hh)),
            pl.BlockSpec((seq, GLA_DK), lambda b, hh: (b, P_GQ // GLA_DK + hh)),
            pl.BlockSpec((seq, GLA_DK), lambda b, hh: (b, P_GK // GLA_DK + hh)),
            pl.BlockSpec((seq, GLA_DV), lambda b, hh: (b, P_GV // GLA_DV + hh)),
            pl.BlockSpec((seq, GLA_DV), lambda b, hh: (b, P_GG // GLA_DV + hh)),
            pl.BlockSpec((1, GLA_DV), lambda b, hh: (0, hh)),
        ],
        out_specs=pl.BlockSpec((seq, GLA_DV), lambda b, hh: (b, hh)),
        out_shape=jax.ShapeDtypeStruct((m, h * GLA_DV), BF16),
        scratch_shapes=[pltpu.VMEM((seq, GLA_DK), F32)],
        compiler_params=_cparams("arbitrary", "arbitrary"),
        name="gla",
    )(ga, w2p, b_alpha, proj, proj, proj, proj, norm_w)


def _swa_slope(hq):
    return 2.0 ** (-8.0 * (hq + 1) / SWA_Q_HEADS)


def _swa_kernel(sink_ref, q_ref, kp_ref, kc_ref, vp_ref, vc_ref, o_ref, bias_ref):
    w = SWA_WINDOW
    dh = SWA_HEAD_DIM
    grp = SWA_Q_HEADS // SWA_KV_HEADS
    n = pl.program_id(1)

    @pl.when((pl.program_id(0) == 0) & (n == 0))
    def _():
        qpos = lax.broadcasted_iota(jnp.int32, (w, 2 * w), 0) + w
        kpos = lax.broadcasted_iota(jnp.int32, (w, 2 * w), 1)
        dist = qpos - kpos
        band = (dist >= 0) & (dist < w)
        distf = dist.astype(F32)
        for hq in range(SWA_Q_HEADS):
            hk, g = divmod(hq, grp)
            alibi = -_swa_slope(hq) * distf
            bias_ref[1, hk, g * w:(g + 1) * w, :] = jnp.where(band, alibi, -jnp.inf)
            bias_ref[0, hk, g * w:(g + 1) * w, :] = jnp.where(band & (kpos >= w), alibi, -jnp.inf)

    sel = jnp.where(n == 0, 0, 1)
    scale = dh ** -0.5
    q = q_ref[...]
    pieces = [None] * SWA_Q_HEADS
    for hk in range(SWA_KV_HEADS):
        sl = slice(hk * dh, (hk + 1) * dh)
        k2 = jnp.concatenate([kp_ref[:, sl], kc_ref[:, sl]], axis=0)
        v2 = jnp.concatenate([vp_ref[:, sl], vc_ref[:, sl]], axis=0)
        qs = jnp.concatenate([q[:, (hk * grp + g) * dh:(hk * grp + g + 1) * dh] for g in range(grp)], axis=0)
        logits = _dot_nt(qs, k2) * scale + bias_ref[sel, hk]
        sink = jnp.concatenate(
            [jnp.full((w, 1), sink_ref[hk * grp + g], F32) for g in range(grp)], axis=0)
        mx = jnp.maximum(jnp.max(logits, axis=-1, keepdims=True), sink)
        p = jnp.exp(logits - mx)
        den = jnp.sum(p, axis=-1, keepdims=True) + jnp.exp(sink - mx)
        o = _dot(p.astype(BF16), v2) / den
        for g in range(grp):
            pieces[hk * grp + g] = o[g * w:(g + 1) * w]
    o_ref[...] = jnp.concatenate(pieces, axis=1).astype(BF16)


def _swa(proj, sinks, batch, seq):
    m = proj.shape[0]
    w = SWA_WINDOW
    nb = seq // w
    qw = SWA_Q_HEADS * SWA_HEAD_DIM
    kw = SWA_KV_HEADS * SWA_HEAD_DIM
    grp = SWA_Q_HEADS // SWA_KV_HEADS

    def cur(col):
        return lambda b, n: (b * nb + n, col)

    def prev(col):
        return lambda b, n: (b * nb + jnp.maximum(n - 1, 0), col)

    return pl.pallas_call(
        _swa_kernel,
        grid=(batch, nb),
        in_specs=[
            pl.BlockSpec(memory_space=pltpu.SMEM),
            pl.BlockSpec((w, qw), cur(P_SQ // qw)),
            pl.BlockSpec((w, kw), prev(P_SK // kw)),
            pl.BlockSpec((w, kw), cur(P_SK // kw)),
            pl.BlockSpec((w, kw), prev(P_SV // kw)),
            pl.BlockSpec((w, kw), cur(P_SV // kw)),
        ],
        out_specs=pl.BlockSpec((w, qw), lambda b, n: (b * nb + n, 0)),
        out_shape=jax.ShapeDtypeStruct((m, qw), BF16),
        scratch_shapes=[pltpu.VMEM((2, SWA_KV_HEADS, grp * w, 2 * w), F32)],
        compiler_params=_cparams("arbitrary", "arbitrary"),
        name="swa",
    )(sinks, proj, proj, proj, proj, proj)


def _merge_kernel(x_ref, gl_ref, r_ref, g_ref, s_ref, wb_ref, wo_ref, nw_ref, o_ref):
    d = D_MODEL
    merged = None
    for n, br in enumerate((r_ref, g_ref, s_ref)):
        p = _dot(br[...], wb_ref[n])
        gate = jax.nn.sigmoid(gl_ref[:, n * d:(n + 1) * d].astype(F32))
        merged = gate * p if merged is None else merged + gate * p
    y = _dot(merged.astype(BF16), wo_ref[...])
    o_ref[...] = x_ref[...] + _rms(y, nw_ref[...], NORM_EPS)


def _merge(x2, proj, o_ret, o_gla, o_swa, wb, wo, nw, tm):
    m, d = x2.shape
    const = pl.Buffered(1)
    return pl.pallas_call(
        _merge_kernel,
        grid=(m // tm,),
        in_specs=[
            pl.BlockSpec((tm, d), lambda i: (i, 0)),
            pl.BlockSpec((tm, N_BRANCH * d), lambda i: (i, P_GATE // (N_BRANCH * d))),
            pl.BlockSpec((tm, d), lambda i: (i, 0)),
            pl.BlockSpec((tm, d), lambda i: (i, 0)),
            pl.BlockSpec((tm, d), lambda i: (i, 0)),
            pl.BlockSpec((N_BRANCH, d, d), lambda i: (0, 0, 0), pipeline_mode=const),
            pl.BlockSpec((d, d), lambda i: (0, 0), pipeline_mode=const),
            pl.BlockSpec((1, d), lambda i: (0, 0)),
        ],
        out_specs=pl.BlockSpec((tm, d), lambda i: (i, 0)),
        out_shape=jax.ShapeDtypeStruct((m, d), F32),
        compiler_params=_cparams("arbitrary"),
        name="merge",
    )(x2, proj, o_ret, o_gla, o_swa, wb, wo, nw)


def _ffn_kernel(x_ref, wg_ref, wu_ref, wd_ref, npre_ref, npost_ref, o_ref):
    x = x_ref[...]
    h = _rms(x, npre_ref[...], NORM_EPS).astype(BF16)
    gate = _dot(h, wg_ref[...])
    up = _dot(h, wu_ref[...])
    act = (gate * jax.nn.sigmoid(gate) * up).astype(BF16)
    f = _dot(act, wd_ref[...])
    o_ref[...] = x + _rms(f, npost_ref[...], NORM_EPS)


def _ffn(x2, wg, wu, wd, npre, npost, tm):
    m, d = x2.shape
    ff = wg.shape[1]
    const = pl.Buffered(1)
    return pl.pallas_call(
        _ffn_kernel,
        grid=(m // tm,),
        in_specs=[
            pl.BlockSpec((tm, d), lambda i: (i, 0)),
            pl.BlockSpec((d, ff), lambda i: (0, 0), pipeline_mode=const),
            pl.BlockSpec((d, ff), lambda i: (0, 0), pipeline_mode=const),
            pl.BlockSpec((ff, d), lambda i: (0, 0), pipeline_mode=const),
            pl.BlockSpec((1, d), lambda i: (0, 0)),
            pl.BlockSpec((1, d), lambda i: (0, 0)),
        ],
        out_specs=pl.BlockSpec((tm, d), lambda i: (i, 0)),
        out_shape=jax.ShapeDtypeStruct((m, d), F32),
        compiler_params=_cparams("arbitrary"),
        name="ffn",
    )(x2, wg, wu, wd, npre, npost)


def _tile(m, want):
    t = min(want, m)
    assert m % t == 0
    return t


def kernel(x, norm_mix_pre, norm_mix_post, w_in, ret_norm_w, ret_norm_b, gla_w_alpha2, gla_b_alpha,
           gla_norm_w, attn_sinks, w_branch, w_out, norm_ffn_pre, norm_ffn_post, ffn_w_gate, ffn_w_up,
           ffn_w_down):
    batch, seq, d = x.shape
    assert d == D_MODEL and seq % SWA_WINDOW == 0
    m = batch * seq
    x2 = x.reshape(m, d)
    log_g = jnp.log1p(-jnp.exp2(-5.0 - jnp.arange(RET_HEADS, dtype=F32)))
    log_g = jnp.repeat(log_g, LANES)[None, :]

    for l in range(DEPTH):
        wl = w_in[l]
        w_main = jnp.concatenate([wl[:, _O_GATE:_O_END], wl[:, _O_RET:_O_GA], wl[:, _O_SWA:_O_GATE]], axis=1).astype(BF16)
        w_ga = jnp.pad(wl[:, _O_GA:_O_SWA], ((0, 0), (0, LANES - GLA_RANK))).astype(BF16)
        w2p = jnp.pad(gla_w_alpha2[l], ((0, LANES - GLA_RANK), (0, 0))).astype(BF16)

        proj, ga = _inproj(x2, norm_mix_pre[l][None, :], w_main, w_ga, _tile(m, 1024), 1792)
        o_ret = _retention(proj, log_g, ret_norm_w[l][None, :], ret_norm_b[l][None, :], batch, seq)
        o_gla = _gla(proj, ga, w2p, gla_b_alpha[l][None, :], gla_norm_w[l][None, :], batch, seq)
        o_swa = _swa(proj, attn_sinks[l], batch, seq)
        x2 = _merge(x2, proj, o_ret, o_gla, o_swa, w_branch[l].astype(BF16), w_out[l].astype(BF16),
                    norm_mix_post[l][None, :], _tile(m, 512))
        x2 = _ffn(x2, ffn_w_gate[l].astype(BF16), ffn_w_up[l].astype(BF16), ffn_w_down[l].astype(BF16),
                  norm_ffn_pre[l][None, :], norm_ffn_post[l][None, :], _tile(m, 512))
    return x2.reshape(batch, seq, d)
```

```python
import functools
import math

import jax
import jax.numpy as jnp
from jax import lax
from jax.experimental import pallas as pl
from jax.experimental.pallas import tpu as pltpu

F32 = jnp.float32
BF16 = jnp.bfloat16

D_MODEL = 1024
DEPTH = 2
RET_HEADS, RET_DK, RET_DV, RET_CHUNK = 4, 128, 256, 128
GLA_HEADS, GLA_DK, GLA_DV, GLA_RANK, GLA_CHUNK = 4, 128, 256, 16, 64
GLA_TAU, GLA_LOG_GATE_MIN = 16.0, -1.0
SWA_Q_HEADS, SWA_KV_HEADS, SWA_HEAD_DIM, SWA_WINDOW = 16, 4, 64, 128
N_BRANCH = 3
D_FF = 2816
NORM_EPS = 1e-6
GN_EPS = 1e-5

LANES = 128
VMEM_LIMIT_BYTES = 56 * 1024 * 1024

_O_RET, _O_GLA, _O_GA, _O_SWA, _O_GATE, _O_END = 0, 3072, 6144, 6160, 7696, 10768
P_GATE = 0
P_RQ, P_RK, P_RV, P_RG = 3072, 3584, 4096, 5120
P_GQ, P_GK, P_GV, P_GG = 6144, 6656, 7168, 8192
P_SQ, P_SK, P_SV = 9216, 10240, 10496
P_TOTAL = 10752


def _cparams(*sem):
    return pltpu.CompilerParams(dimension_semantics=sem, vmem_limit_bytes=VMEM_LIMIT_BYTES)


def _rms(x, w, eps):
    return x * lax.rsqrt(jnp.mean(x * x, axis=-1, keepdims=True) + eps) * w


def _dot(a, b):
    return jnp.dot(a, b, preferred_element_type=F32)


def _dot_nt(a, b):
    return lax.dot_general(a, b, (((1,), (1,)), ((), ())), preferred_element_type=F32)


def _silu_bf16(g):
    h = g * 0.5
    return h + h * jnp.tanh(h)


def _dot_tn(a, b):
    return lax.dot_general(a, b, (((0,), (0,)), ((), ())), preferred_element_type=F32)


def _pipeline3(n, first, middle, last, state, lanes):
    assert n % lanes == 0
    rounds = n // lanes

    def run_round(r, state, do_last, do_middle, do_first):
        for lane in range(lanes):
            if do_last:
                last(r - 2, lane)
            if do_middle:
                state = middle(r - 1, lane, state)
            if do_first:
                first(r, lane)
        return state

    for r in range(min(2, rounds + 2)):
        state = run_round(r, state, False, 0 <= r - 1 < rounds, r < rounds)
    if rounds > 2:
        state = lax.fori_loop(2, rounds, lambda r, s: run_round(r, s, True, True, True), state)
    for r in range(max(rounds, 2), rounds + 2):
        state = run_round(r, state, r - 2 < rounds, r - 1 < rounds, False)
    return state


def _pipeline3_ring(n, first, middle, last):
    def run_round(r, par):
        static = isinstance(r, int)
        if not static or r < n:
            first(r, par)
        if not static or 0 <= r - 1 < n:
            middle(r - 1, 1 - par)
        if not static or 0 <= r - 2 < n:
            last(r - 2, par)

    lo = 2
    hi = max(lo, n - (n - lo) % 2)
    for r in range(min(lo, n + 2)):
        run_round(r, r % 2)
    if hi > lo:
        def two_rounds(j, carry):
            r = lo + 2 * j
            run_round(r, 0)
            run_round(r + 1, 1)
            return carry

        lax.fori_loop(0, (hi - lo) // 2, two_rounds, 0)
    for r in range(hi, n + 2):
        run_round(r, r % 2)


def _inproj_kernel(x_ref, g_ref, w_ref, wga_ref, o_ref, oga_ref, h_ref):
    @pl.when(pl.program_id(1) == 0)
    def _():
        h = _rms(x_ref[...], g_ref[...], NORM_EPS).astype(BF16)
        h_ref[...] = h
        oga_ref[...] = _dot(h, wga_ref[...]).astype(BF16)

    o_ref[...] = _dot(h_ref[...], w_ref[...]).astype(BF16)


def _inproj(x2, gain, w_main, w_ga, tm, tn):
    m, d = x2.shape
    n = w_main.shape[1]
    return pl.pallas_call(
        _inproj_kernel,
        grid=(m // tm, n // tn),
        in_specs=[
            pl.BlockSpec((tm, d), lambda i, j: (i, 0)),
            pl.BlockSpec((1, d), lambda i, j: (0, 0)),
            pl.BlockSpec((d, tn), lambda i, j: (0, j)),
            pl.BlockSpec((d, LANES), lambda i, j: (0, 0)),
        ],
        out_specs=[
            pl.BlockSpec((tm, tn), lambda i, j: (i, j)),
            pl.BlockSpec((tm, LANES), lambda i, j: (i, 0)),
        ],
        out_shape=[jax.ShapeDtypeStruct((m, n), BF16), jax.ShapeDtypeStruct((m, LANES), BF16)],
        scratch_shapes=[pltpu.VMEM((tm, d), BF16)],
        compiler_params=_cparams("arbitrary", "arbitrary"),
        name="inproj",
    )(x2, gain, w_main, w_ga)


RET_LANES = 4


def _ret_kernel(lg_ref, q_ref, k_ref, v_ref, g_ref, nw_ref, nb_ref, o_ref, *scratch, n_chunks):
    c = RET_CHUNK
    lg = lg_ref[...]
    row = lax.broadcasted_iota(jnp.int32, (c, c), 0)
    col = lax.broadcasted_iota(jnp.int32, (c, c), 1)
    rel = (row - col).astype(F32)
    scale = RET_DK ** -0.5
    decay = jnp.where(row >= col, jnp.exp(rel * lg), 0.0) * scale
    rowf = row.astype(F32)
    xi = (jnp.exp((rowf + 1.0) * lg) * scale).astype(BF16)
    zeta = jnp.exp((c - 1.0 - rowf) * lg).astype(BF16)
    gamma_c = jnp.exp(c * lg)
    gamma_c = jnp.concatenate([gamma_c, gamma_c], axis=1)
    nw = nw_ref[...].astype(BF16)
    nb = nb_ref[...].astype(BF16)

    lhs_refs, kv_refs, acc_refs = scratch[0::3], scratch[1::3], scratch[2::3]

    def rows(slot, lane):
        return pl.ds(pl.multiple_of((slot * RET_LANES + lane) * c, c), c)

    def scores(slot, lane):
        r = rows(slot, lane)
        qc = q_ref[r, :]
        kc = k_ref[r, :]
        s = _dot_nt(qc, kc) * decay
        lhs_refs[lane][slot] = jnp.concatenate([s.astype(BF16), qc * xi], axis=1)
        kv_refs[lane][slot] = _dot_tn(kc * zeta, v_ref[r, :])

    def mix(slot, lane, state):
        rhs = jnp.concatenate([v_ref[rows(slot, lane), :], state.astype(BF16)], axis=0)
        acc_refs[lane][slot] = _dot(lhs_refs[lane][slot], rhs)
        return gamma_c * state + kv_refs[lane][slot]

    def finish(slot, lane):
        r = rows(slot, lane)
        o = acc_refs[lane][slot]
        mu = jnp.mean(o, axis=-1, keepdims=True)
        d = o - mu
        var = jnp.mean(d * d, axis=-1, keepdims=True)
        y = (d * lax.rsqrt(var + GN_EPS)).astype(BF16)
        o_ref[r, :] = (y * nw + nb) * _silu_bf16(g_ref[r, :])

    _pipeline3(n_chunks, scores, mix, finish, jnp.zeros((RET_DK, RET_DV), F32), RET_LANES)


def _retention(proj, log_g, norm_w, norm_b, batch, seq):
    m = proj.shape[0]
    h = RET_HEADS
    n_chunks = seq // RET_CHUNK
    kern = functools.partial(_ret_kernel, n_chunks=n_chunks)
    return pl.pallas_call(
        kern,
        grid=(batch, h),
        in_specs=[
            pl.BlockSpec((1, LANES), lambda b, hh: (0, hh)),
            pl.BlockSpec((seq, RET_DK), lambda b, hh: (b, P_RQ // RET_DK + hh)),
            pl.BlockSpec((seq, RET_DK), lambda b, hh: (b, P_RK // RET_DK + hh)),
            pl.BlockSpec((seq, RET_DV), lambda b, hh: (b, P_RV // RET_DV + hh)),
            pl.BlockSpec((seq, RET_DV), lambda b, hh: (b, P_RG // RET_DV + hh)),
            pl.BlockSpec((1, RET_DV), lambda b, hh: (0, hh)),
            pl.BlockSpec((1, RET_DV), lambda b, hh: (0, hh)),
        ],
        out_specs=pl.BlockSpec((seq, RET_DV), lambda b, hh: (b, hh)),
        out_shape=jax.ShapeDtypeStruct((m, h * RET_DV), BF16),
        scratch_shapes=[
            pltpu.VMEM((n_chunks // RET_LANES, RET_CHUNK, RET_CHUNK + RET_DK), BF16),
            pltpu.VMEM((n_chunks // RET_LANES, RET_DK, RET_DV), F32),
            pltpu.VMEM((n_chunks // RET_LANES, RET_CHUNK, RET_DV), F32),
        ] * RET_LANES,
        compiler_params=_cparams("arbitrary", "arbitrary"),
        name="retention",
    )(log_g, proj, proj, proj, proj, norm_w, norm_b)


GLA_LANES = 4


def _gla_kernel(ga_ref, w2_ref, ba_ref, q_ref, k_ref, v_ref, g_ref, nw_ref, o_ref, la_ref, *scratch, n_pairs):
    c = GLA_CHUNK
    c2 = 2 * c
    dk = GLA_DK
    z = _dot(ga_ref[...], w2_ref[...]) + ba_ref[...]
    log_sig = jnp.minimum(z, 0.0) - jnp.log(1.0 + jnp.exp(-jnp.abs(z)))
    la_ref[...] = jnp.maximum(log_sig * (1.0 / GLA_TAU), GLA_LOG_GATE_MIN)

    row = lax.broadcasted_iota(jnp.int32, (c2, c2), 0)
    col = lax.broadcasted_iota(jnp.int32, (c2, c2), 1)
    causal = (row >= col) & ((row >= c) == (col >= c))
    tri = jnp.where(causal, 1.0, 0.0).astype(BF16)
    scale = dk ** -0.5
    nw = nw_ref[...].astype(BF16)

    qd_ref, ki_ref, kd_ref, dl_ref = scratch[:4]
    a_refs, kv_refs, dc_refs, sb_refs = (scratch[4 + j::4] for j in range(4))

    def decays(p, carry):
        r = pl.ds(pl.multiple_of(p * c2, c2), c2)
        la = la_ref[r, :]
        hi = la.astype(BF16)
        rest = la - hi.astype(F32)
        mid = rest.astype(BF16)
        low = (rest - mid.astype(F32)).astype(BF16)
        parts = _dot(tri, jnp.concatenate([hi, mid, low], axis=1))
        bcum = parts[:, :dk] + parts[:, dk:2 * dk] + parts[:, 2 * dk:]
        eb = jnp.exp(bcum)
        k_inv = k_ref[r, :].astype(F32) * (1.0 / eb)
        qd_ref[r, :] = (q_ref[r, :].astype(F32) * (eb * scale)).astype(BF16)
        ki_ref[r, :] = k_inv.astype(BF16)
        k_dec = []
        for half in range(2):
            lo, up = half * c, (half + 1) * c
            d_last = eb[up - 1:up, :]
            dl_ref[p, half] = jnp.broadcast_to(d_last, (8, dk))
            k_dec.append((k_inv[lo:up] * d_last).astype(BF16))
        kd_ref[r, :] = jnp.concatenate(k_dec, axis=0)
        return carry

    lax.fori_loop(0, n_pairs, decays, 0, unroll=4)

    def rows(slot, lane):
        return pl.ds(pl.multiple_of((slot * GLA_LANES + lane) * c2, c2), c2)

    def prepare(slot, lane):
        r = rows(slot, lane)
        a = _dot_nt(qd_ref[r, :], ki_ref[r, :])
        a_refs[lane][slot] = jnp.where(causal, a, 0.0).astype(BF16)
        k_dec = kd_ref[r, :]
        vc = v_ref[r, :]
        for half in range(2):
            lo, up = half * c, (half + 1) * c
            kv_refs[lane][slot, half] = _dot_tn(k_dec[lo:up], vc[lo:up])
            d_last = dl_ref[slot * GLA_LANES + lane, half][0:1, :]
            dc_refs[lane][slot, half] = jnp.broadcast_to(d_last, (dk, dk)).T

    def scan(slot, lane, state):
        for half in range(2):
            sb_refs[lane][slot, half] = state.astype(BF16)
            dcol = dc_refs[lane][slot, half]
            state = jnp.concatenate([dcol, dcol], axis=1) * state + kv_refs[lane][slot, half]
        return state

    def finish(slot, lane):
        r = rows(slot, lane)
        q_dec = qd_ref[r, :]
        intra = _dot(a_refs[lane][slot], v_ref[r, :])
        o = jnp.concatenate(
            [intra[h * c:(h + 1) * c] + _dot(q_dec[h * c:(h + 1) * c], sb_refs[lane][slot, h]) for h in range(2)],
            axis=0)
        y = (o * lax.rsqrt(jnp.mean(o * o, axis=-1, keepdims=True) + NORM_EPS)).astype(BF16)
        o_ref[r, :] = (y * nw) * _silu_bf16(g_ref[r, :])

    _pipeline3(n_pairs, prepare, scan, finish, jnp.zeros((dk, GLA_DV), F32), GLA_LANES)


def _gla(proj, ga, w2p, b_alpha, norm_w, batch, seq):
    m = proj.shape[0]
    h = GLA_HEADS
    n_pairs = seq // (2 * GLA_CHUNK)
    slots = n_pairs // GLA_LANES
    kern = functools.partial(_gla_kernel, n_pairs=n_pairs)
    return pl.pallas_call(
        kern,
        grid=(batch, h),
        in_specs=[
            pl.BlockSpec((seq, LANES), lambda b, hh: (b, 0)),
            pl.BlockSpec((LANES, GLA_DK), lambda b, hh: (0, hh)),
            pl.BlockSpec((1, GLA_DK), lambda b, hh: (0, hh)),
            pl.BlockSpec((seq, GLA_DK), lambda b, hh: (b, P_GQ // GLA_DK + hh)),
            pl.BlockSpec((seq, GLA_DK), lambda b, hh: (b, P_GK // GLA_DK + hh)),
            pl.BlockSpec((seq, GLA_DV), lambda b, hh: (b, P_GV // GLA_DV + hh)),
            pl.BlockSpec((seq, GLA_DV), lambda b, hh: (b, P_GG // GLA_DV + hh)),
            pl.BlockSpec((1, GLA_DV), lambda b, hh: (0, hh)),
        ],
        out_specs=pl.BlockSpec((seq, GLA_DV), lambda b, hh: (b, hh)),
        out_shape=jax.ShapeDtypeStruct((m, h * GLA_DV), BF16),
        scratch_shapes=[
            pltpu.VMEM((seq, GLA_DK), F32),
            pltpu.VMEM((seq, GLA_DK), BF16),
            pltpu.VMEM((seq, GLA_DK), BF16),
            pltpu.VMEM((seq, GLA_DK), BF16),
            pltpu.VMEM((n_pairs, 2, 8, GLA_DK), F32),
        ] + [
            pltpu.VMEM((slots, 2 * GLA_CHUNK, 2 * GLA_CHUNK), BF16),
            pltpu.VMEM((slots, 2, GLA_DK, GLA_DV), F32),
            pltpu.VMEM((slots, 2, GLA_DK, GLA_DK), F32),
            pltpu.VMEM((slots, 2, GLA_DK, GLA_DV), BF16),
        ] * GLA_LANES,
        compiler_params=_cparams("arbitrary", "arbitrary"),
        name="gla",
    )(ga, w2p, b_alpha, proj, proj, proj, proj, norm_w)


def _swa_slope(hq):
    return 2.0 ** (-8.0 * (hq + 1) / SWA_Q_HEADS)


SWA_GROUP = SWA_Q_HEADS // SWA_KV_HEADS
SWA_PAIRS = SWA_KV_HEADS // 2


def _swa_head(pair, half, g):
    return (2 * pair + half) * SWA_GROUP + g


def _swa_head_order():
    return [_swa_head(pair, half, g) for pair in range(SWA_PAIRS) for g in range(SWA_GROUP) for half in range(2)]


def _swa_kernel(sink_ref, q_ref, k_ref, v_ref, o_ref, bias_ref, lg_ref, acc_ref, es_ref, *, n_blocks):
    w = SWA_WINDOW
    dh = SWA_HEAD_DIM
    gw = SWA_GROUP * w
    nkv = SWA_KV_HEADS

    @pl.when(pl.program_id(0) == 0)
    def _():
        kpos = lax.broadcasted_iota(jnp.int32, (2 * w, w), 0)
        qpos = lax.broadcasted_iota(jnp.int32, (2 * w, w), 1) + w
        dist = qpos - kpos
        band = (dist >= 0) & (dist < w)
        distf = dist.astype(F32)
        for pair in range(SWA_PAIRS):
            for half in range(2):
                for g in range(SWA_GROUP):
                    alibi = -_swa_slope(_swa_head(pair, half, g)) * distf
                    kv = 2 * pair + half
                    bias_ref[kv, :, g * w:(g + 1) * w] = jnp.where(band, alibi, -jnp.inf)
                    bias_ref[nkv + kv, :, g * w:(g + 1) * w] = jnp.where(band & (kpos >= w), alibi, -jnp.inf)

    lane = lax.broadcasted_iota(jnp.int32, (1, 2 * dh), 1)
    in_half = [lane < dh, lane >= dh]
    sub = lax.broadcasted_iota(jnp.int32, (2 * dh, 1), 0)
    sinks = [jnp.concatenate([jnp.full((1, w), sink_ref[_swa_head(kv // 2, kv % 2, g)], F32)
                              for g in range(SWA_GROUP)], axis=1) for kv in range(nkv)]

    def block_rows(n):
        return pl.ds(n * w if isinstance(n, int) else pl.multiple_of(n * w, w), w)

    q_rows = block_rows

    def kv_block(ref, n, cols):
        prev = max(n - 1, 0) if isinstance(n, int) else n - 1
        return jnp.concatenate([ref[block_rows(prev), cols], ref[block_rows(n), cols]], axis=0)

    def scores(n, par):
        table = nkv if isinstance(n, int) and n == 0 else 0
        for pair in range(SWA_PAIRS):
            qst = jnp.concatenate(
                [q_ref[q_rows(n), (pair * SWA_GROUP + g) * 2 * dh:(pair * SWA_GROUP + g + 1) * 2 * dh]
                 for g in range(SWA_GROUP)], axis=0)
            kp = kv_block(k_ref, n, slice(pair * 2 * dh, (pair + 1) * 2 * dh)) * (dh ** -0.5)
            for half in range(2):
                kv = 2 * pair + half
                kh = jnp.where(in_half[half], kp, jnp.zeros_like(kp))
                lg_ref[par, kv] = _dot_nt(kh, qst) + bias_ref[table + kv]

    def probs(n, par):
        for pair in range(SWA_PAIRS):
            vp = kv_block(v_ref, n, slice(pair * 2 * dh, (pair + 1) * 2 * dh))
            for half in range(2):
                kv = 2 * pair + half
                logits = lg_ref[par, kv]
                mx = jnp.maximum(jnp.max(logits, axis=0, keepdims=True), sinks[kv])
                p = jnp.exp(logits - mx).astype(BF16)
                vh = jnp.where(in_half[half], vp, jnp.ones_like(vp))
                acc_ref[par, kv] = _dot_tn(vh, p)
                es_ref[par, kv] = jnp.broadcast_to(jnp.exp(sinks[kv] - mx), (8, gw))

    def outputs(n, par):
        for pair in range(SWA_PAIRS):
            out_t = None
            for half in range(2):
                kv = 2 * pair + half
                acc = acc_ref[par, kv]
                other = (1 - half) * dh
                den = acc[other:other + 1, :] + es_ref[par, kv][0:1, :]
                acc = acc * (1.0 / den)
                out_t = acc if out_t is None else jnp.where(sub < dh, out_t, acc)
            for g in range(SWA_GROUP):
                blk = pair * SWA_GROUP + g
                o_ref[q_rows(n), blk * 2 * dh:(blk + 1) * 2 * dh] = out_t[:, g * w:(g + 1) * w].T.astype(BF16)

    _pipeline3_ring(n_blocks, scores, probs, outputs)


def _swa(proj, sinks, batch, seq):
    m = proj.shape[0]
    w = SWA_WINDOW
    qw = SWA_Q_HEADS * SWA_HEAD_DIM
    kw = SWA_KV_HEADS * SWA_HEAD_DIM
    kern = functools.partial(_swa_kernel, n_blocks=seq // w)
    return pl.pallas_call(
        kern,
        grid=(batch,),
        in_specs=[
            pl.BlockSpec(memory_space=pltpu.SMEM),
            pl.BlockSpec((seq, qw), lambda b: (b, P_SQ // qw)),
            pl.BlockSpec((seq, kw), lambda b: (b, P_SK // kw)),
            pl.BlockSpec((seq, kw), lambda b: (b, P_SV // kw)),
        ],
        out_specs=pl.BlockSpec((seq, qw), lambda b: (b, 0)),
        out_shape=jax.ShapeDtypeStruct((m, qw), BF16),
        scratch_shapes=[
            pltpu.VMEM((2 * SWA_KV_HEADS, 2 * w, SWA_GROUP * w), F32),
            pltpu.VMEM((2, SWA_KV_HEADS, 2 * w, SWA_GROUP * w), F32),
            pltpu.VMEM((2, SWA_KV_HEADS, 2 * SWA_HEAD_DIM, SWA_GROUP * w), F32),
            pltpu.VMEM((2, SWA_KV_HEADS, 8, SWA_GROUP * w), F32),
        ],
        compiler_params=_cparams("arbitrary"),
        name="swa",
    )(sinks, proj, proj, proj)


def _merge_kernel(x_ref, gl_ref, r_ref, g_ref, s_ref, wb_ref, wo_ref, nw_ref, o_ref):
    d = D_MODEL
    merged = None
    for n, br in enumerate((r_ref, g_ref, s_ref)):
        p = _dot(br[...], wb_ref[n])
        gate = jax.nn.sigmoid(gl_ref[:, n * d:(n + 1) * d].astype(F32))
        merged = gate * p if merged is None else merged + gate * p
    y = _dot(merged.astype(BF16), wo_ref[...])
    o_ref[...] = x_ref[...] + _rms(y, nw_ref[...], NORM_EPS)


def _merge(x2, proj, o_ret, o_gla, o_swa, wb, wo, nw, tm):
    m, d = x2.shape
    const = pl.Buffered(1)
    return pl.pallas_call(
        _merge_kernel,
        grid=(m // tm,),
        in_specs=[
            pl.BlockSpec((tm, d), lambda i: (i, 0)),
            pl.BlockSpec((tm, N_BRANCH * d), lambda i: (i, P_GATE // (N_BRANCH * d))),
            pl.BlockSpec((tm, d), lambda i: (i, 0)),
            pl.BlockSpec((tm, d), lambda i: (i, 0)),
            pl.BlockSpec((tm, d), lambda i: (i, 0)),
            pl.BlockSpec((N_BRANCH, d, d), lambda i: (0, 0, 0), pipeline_mode=const),
            pl.BlockSpec((d, d), lambda i: (0, 0), pipeline_mode=const),
            pl.BlockSpec((1, d), lambda i: (0, 0)),
        ],
        out_specs=pl.BlockSpec((tm, d), lambda i: (i, 0)),
        out_shape=jax.ShapeDtypeStruct((m, d), F32),
        compiler_params=_cparams("arbitrary"),
        name="merge",
    )(x2, proj, o_ret, o_gla, o_swa, wb, wo, nw)


def _ffn_kernel(x_ref, wg_ref, wu_ref, wd_ref, npre_ref, npost_ref, o_ref):
    x = x_ref[...]
    h = _rms(x, npre_ref[...], NORM_EPS).astype(BF16)
    gate = _dot(h, wg_ref[...])
    up = _dot(h, wu_ref[...])
    act = (gate * jax.nn.sigmoid(gate) * up).astype(BF16)
    f = _dot(act, wd_ref[...])
    o_ref[...] = x + _rms(f, npost_ref[...], NORM_EPS)


def _ffn(x2, wg, wu, wd, npre, npost, tm):
    m, d = x2.shape
    ff = wg.shape[1]
    const = pl.Buffered(1)
    return pl.pallas_call(
        _ffn_kernel,
        grid=(m // tm,),
        in_specs=[
            pl.BlockSpec((tm, d), lambda i: (i, 0)),
            pl.BlockSpec((d, ff), lambda i: (0, 0), pipeline_mode=const),
            pl.BlockSpec((d, ff), lambda i: (0, 0), pipeline_mode=const),
            pl.BlockSpec((ff, d), lambda i: (0, 0), pipeline_mode=const),
            pl.BlockSpec((1, d), lambda i: (0, 0)),
            pl.BlockSpec((1, d), lambda i: (0, 0)),
        ],
        out_specs=pl.BlockSpec((tm, d), lambda i: (i, 0)),
        out_shape=jax.ShapeDtypeStruct((m, d), F32),
        compiler_params=_cparams("arbitrary"),
        name="ffn",
    )(x2, wg, wu, wd, npre, npost)


def _tile(m, want):
    t = min(want, m)
    assert m % t == 0
    return t


def kernel(x, norm_mix_pre, norm_mix_post, w_in, ret_norm_w, ret_norm_b, gla_w_alpha2, gla_b_alpha,
           gla_norm_w, attn_sinks, w_branch, w_out, norm_ffn_pre, norm_ffn_post, ffn_w_gate, ffn_w_up,
           ffn_w_down):
    batch, seq, d = x.shape
    assert d == D_MODEL and seq % SWA_WINDOW == 0
    m = batch * seq
    x2 = x.reshape(m, d)
    log_g = jnp.log1p(-jnp.exp2(-5.0 - jnp.arange(RET_HEADS, dtype=F32)))
    log_g = jnp.repeat(log_g, LANES)[None, :]

    swa_perm = jnp.asarray(
        [hq * SWA_HEAD_DIM + j for hq in _swa_head_order() for j in range(SWA_HEAD_DIM)], dtype=jnp.int32)
    o_sk = _O_SWA + SWA_Q_HEADS * SWA_HEAD_DIM

    for l in range(DEPTH):
        wl = w_in[l]
        w_main = jnp.concatenate(
            [wl[:, _O_GATE:_O_END], wl[:, _O_RET:_O_GA], wl[:, _O_SWA:o_sk][:, swa_perm], wl[:, o_sk:_O_GATE]],
            axis=1).astype(BF16)
        w_br = w_branch[l].astype(BF16)
        w_br = w_br.at[2].set(w_br[2][swa_perm, :])
        w_ga = jnp.pad(wl[:, _O_GA:_O_SWA], ((0, 0), (0, LANES - GLA_RANK))).astype(BF16)
        w2p = jnp.pad(gla_w_alpha2[l], ((0, LANES - GLA_RANK), (0, 0))).astype(BF16)

        proj, ga = _inproj(x2, norm_mix_pre[l][None, :], w_main, w_ga, _tile(m, 1024), 1792)
        o_ret = _retention(proj, log_g, ret_norm_w[l][None, :], ret_norm_b[l][None, :], batch, seq)
        o_gla = _gla(proj, ga, w2p, gla_b_alpha[l][None, :], gla_norm_w[l][None, :], batch, seq)
        o_swa = _swa(proj, attn_sinks[l], batch, seq)
        x2 = _merge(x2, proj, o_ret, o_gla, o_swa, w_br, w_out[l].astype(BF16),
                    norm_mix_post[l][None, :], _tile(m, 512))
        x2 = _ffn(x2, ffn_w_gate[l].astype(BF16), ffn_w_up[l].astype(BF16), ffn_w_down[l].astype(BF16),
                  norm_ffn_pre[l][None, :], norm_ffn_post[l][None, :], _tile(m, 512))
    return x2.reshape(batch, seq, d)
```

```python
import functools

import jax
import jax.numpy as jnp
from jax import lax
from jax.experimental import pallas as pl
from jax.experimental.pallas import tpu as pltpu

F32 = jnp.float32
BF16 = jnp.bfloat16

D_MODEL = 1024
DEPTH = 2
RET_HEADS, RET_DK, RET_DV, RET_CHUNK = 4, 128, 256, 128
GLA_HEADS, GLA_DK, GLA_DV, GLA_RANK, GLA_CHUNK = 4, 128, 256, 16, 64
GLA_TAU, GLA_LOG_GATE_MIN = 16.0, -1.0
SWA_Q_HEADS, SWA_KV_HEADS, SWA_HEAD_DIM, SWA_WINDOW = 16, 4, 64, 128
N_BRANCH = 3
D_FF = 2816
NORM_EPS = 1e-6
GN_EPS = 1e-5

LANES = 128
VMEM_LIMIT_BYTES = 56 * 1024 * 1024

_O_RET, _O_GLA, _O_GA, _O_SWA, _O_GATE, _O_END = 0, 3072, 6144, 6160, 7696, 10768
P_GATE = 0
P_RQ, P_RK, P_RV, P_RG = 3072, 3584, 4096, 5120
P_GQ, P_GK, P_GV, P_GG = 6144, 6656, 7168, 8192
P_SQ, P_SK, P_SV = 9216, 10240, 10496
P_TOTAL = 10752


def _cparams(*sem):
    return pltpu.CompilerParams(dimension_semantics=sem, vmem_limit_bytes=VMEM_LIMIT_BYTES)


def _rms(x, w, eps):
    return x * lax.rsqrt(jnp.mean(x * x, axis=-1, keepdims=True) + eps) * w


def _dot(a, b):
    return jnp.dot(a, b, preferred_element_type=F32)


def _dot_nt(a, b):
    return lax.dot_general(a, b, (((1,), (1,)), ((), ())), preferred_element_type=F32)


def _silu_bf16(g):
    h = g * 0.5
    return h + h * jnp.tanh(h)


def _dot_tn(a, b):
    return lax.dot_general(a, b, (((0,), (0,)), ((), ())), preferred_element_type=F32)


def _pipeline3(n, first, middle, last, state, lanes):
    assert n % lanes == 0
    rounds = n // lanes

    def run_round(r, state, do_last, do_middle, do_first):
        for lane in range(lanes):
            if do_last:
                last(r - 2, lane)
            if do_middle:
                state = middle(r - 1, lane, state)
            if do_first:
                first(r, lane)
        return state

    for r in range(min(2, rounds + 2)):
        state = run_round(r, state, False, 0 <= r - 1 < rounds, r < rounds)
    if rounds > 2:
        state = lax.fori_loop(2, rounds, lambda r, s: run_round(r, s, True, True, True), state)
    for r in range(max(rounds, 2), rounds + 2):
        state = run_round(r, state, r - 2 < rounds, r - 1 < rounds, False)
    return state


def _pipeline3_ring(n, first, middle, last):
    def run_round(r, par):
        static = isinstance(r, int)
        if not static or r < n:
            first(r, par)
        if not static or 0 <= r - 1 < n:
            middle(r - 1, 1 - par)
        if not static or 0 <= r - 2 < n:
            last(r - 2, par)

    lo = 2
    hi = max(lo, n - (n - lo) % 2)
    for r in range(min(lo, n + 2)):
        run_round(r, r % 2)
    if hi > lo:
        def two_rounds(j, carry):
            r = lo + 2 * j
            run_round(r, 0)
            run_round(r + 1, 1)
            return carry

        lax.fori_loop(0, (hi - lo) // 2, two_rounds, 0)
    for r in range(hi, n + 2):
        run_round(r, r % 2)


def _inproj_kernel(x_ref, g_ref, w_ref, wga_ref, o_ref, oga_ref, h_ref):
    @pl.when(pl.program_id(1) == 0)
    def _():
        h = _rms(x_ref[...], g_ref[...], NORM_EPS).astype(BF16)
        h_ref[...] = h
        oga_ref[...] = _dot(h, wga_ref[...]).astype(BF16)

    o_ref[...] = _dot(h_ref[...], w_ref[...]).astype(BF16)


def _inproj(x2, gain, w_main, w_ga, layer, tm, tn):
    m, d = x2.shape
    n = w_main.shape[2]
    return pl.pallas_call(
        _inproj_kernel,
        grid=(m // tm, n // tn),
        in_specs=[
            pl.BlockSpec((tm, d), lambda i, j: (i, 0)),
            pl.BlockSpec((1, d), lambda i, j: (0, 0)),
            pl.BlockSpec((None, d, tn), lambda i, j: (layer, 0, j)),
            pl.BlockSpec((None, d, LANES), lambda i, j: (layer, 0, 0)),
        ],
        out_specs=[
            pl.BlockSpec((tm, tn), lambda i, j: (i, j)),
            pl.BlockSpec((tm, LANES), lambda i, j: (i, 0)),
        ],
        out_shape=[jax.ShapeDtypeStruct((m, n), BF16), jax.ShapeDtypeStruct((m, LANES), BF16)],
        scratch_shapes=[pltpu.VMEM((tm, d), BF16)],
        compiler_params=_cparams("arbitrary", "arbitrary"),
        name="inproj",
    )(x2, gain, w_main, w_ga)


RET_LANES = 8


def _ret_kernel(lg_ref, q_ref, k_ref, v_ref, g_ref, nw_ref, nb_ref, o_ref, *scratch, n_chunks):
    c = RET_CHUNK
    lg = lg_ref[...]
    row = lax.broadcasted_iota(jnp.int32, (c, c), 0)
    col = lax.broadcasted_iota(jnp.int32, (c, c), 1)
    rel = (row - col).astype(F32)
    scale = RET_DK ** -0.5
    decay = jnp.where(row >= col, jnp.exp(rel * lg), 0.0) * scale
    rowf = row.astype(F32)
    xi = (jnp.exp((rowf + 1.0) * lg) * scale).astype(BF16)
    zeta = jnp.exp((c - 1.0 - rowf) * lg).astype(BF16)
    gamma_c = jnp.exp(c * lg)
    gamma_c = jnp.concatenate([gamma_c, gamma_c], axis=1)
    nw = nw_ref[...].astype(BF16)
    nb = nb_ref[...].astype(BF16)

    lhs_refs, kv_refs, acc_refs = scratch[0::3], scratch[1::3], scratch[2::3]

    def rows(slot, lane):
        return pl.ds(pl.multiple_of((slot * RET_LANES + lane) * c, c), c)

    def scores(slot, lane):
        r = rows(slot, lane)
        qc = q_ref[r, :]
        kc = k_ref[r, :]
        s = _dot_nt(qc, kc) * decay
        lhs_refs[lane][slot] = jnp.concatenate([s.astype(BF16), qc * xi], axis=1)
        kv_refs[lane][slot] = _dot_tn(kc * zeta, v_ref[r, :])

    def mix(slot, lane, state):
        rhs = jnp.concatenate([v_ref[rows(slot, lane), :], state.astype(BF16)], axis=0)
        acc_refs[lane][slot] = _dot(lhs_refs[lane][slot], rhs)
        return gamma_c * state + kv_refs[lane][slot]

    def finish(slot, lane):
        r = rows(slot, lane)
        o = acc_refs[lane][slot]
        mu = jnp.mean(o, axis=-1, keepdims=True)
        d = o - mu
        var = jnp.mean(d * d, axis=-1, keepdims=True)
        y = (d * lax.rsqrt(var + GN_EPS)).astype(BF16)
        o_ref[r, :] = (y * nw + nb) * _silu_bf16(g_ref[r, :])

    _pipeline3(n_chunks, scores, mix, finish, jnp.zeros((RET_DK, RET_DV), F32), RET_LANES)


def _retention(proj, log_g, norm_w, norm_b, batch, seq):
    m = proj.shape[0]
    h = RET_HEADS
    n_chunks = seq // RET_CHUNK
    kern = functools.partial(_ret_kernel, n_chunks=n_chunks)
    return pl.pallas_call(
        kern,
        grid=(batch, h),
        in_specs=[
            pl.BlockSpec((1, LANES), lambda b, hh: (0, hh)),
            pl.BlockSpec((seq, RET_DK), lambda b, hh: (b, P_RQ // RET_DK + hh)),
            pl.BlockSpec((seq, RET_DK), lambda b, hh: (b, P_RK // RET_DK + hh)),
            pl.BlockSpec((seq, RET_DV), lambda b, hh: (b, P_RV // RET_DV + hh)),
            pl.BlockSpec((seq, RET_DV), lambda b, hh: (b, P_RG // RET_DV + hh)),
            pl.BlockSpec((1, RET_DV), lambda b, hh: (0, hh)),
            pl.BlockSpec((1, RET_DV), lambda b, hh: (0, hh)),
        ],
        out_specs=pl.BlockSpec((seq, RET_DV), lambda b, hh: (b, hh)),
        out_shape=jax.ShapeDtypeStruct((m, h * RET_DV), BF16),
        scratch_shapes=[
            pltpu.VMEM((n_chunks // RET_LANES, RET_CHUNK, RET_CHUNK + RET_DK), BF16),
            pltpu.VMEM((n_chunks // RET_LANES, RET_DK, RET_DV), F32),
            pltpu.VMEM((n_chunks // RET_LANES, RET_CHUNK, RET_DV), F32),
        ] * RET_LANES,
        compiler_params=_cparams("arbitrary", "arbitrary"),
        name="retention",
    )(log_g, proj, proj, proj, proj, norm_w, norm_b)


GLA_LANES = 8


def _gla_kernel(ga_ref, w2_ref, ba_ref, q_ref, k_ref, v_ref, g_ref, nw_ref, o_ref, la_ref, *scratch, n_pairs):
    c = GLA_CHUNK
    c2 = 2 * c
    dk = GLA_DK
    z = _dot(ga_ref[...], w2_ref[...]) + ba_ref[...]
    log_sig = jnp.minimum(z, 0.0) - jnp.log(1.0 + jnp.exp(-jnp.abs(z)))
    la_ref[...] = jnp.maximum(log_sig * (1.0 / GLA_TAU), GLA_LOG_GATE_MIN)

    row = lax.broadcasted_iota(jnp.int32, (c2, c2), 0)
    col = lax.broadcasted_iota(jnp.int32, (c2, c2), 1)
    causal = (row >= col) & ((row >= c) == (col >= c))
    tri = jnp.where(causal, 1.0, 0.0).astype(BF16)
    scale = dk ** -0.5
    nw = nw_ref[...].astype(BF16)

    qd_ref, ki_ref, kd_ref, dl_ref = scratch[:4]
    a_refs, kv_refs, dc_refs, sb_refs = (scratch[4 + j::4] for j in range(4))

    def decays(p, carry):
        r = pl.ds(pl.multiple_of(p * c2, c2), c2)
        la = la_ref[r, :]
        hi = la.astype(BF16)
        rest = la - hi.astype(F32)
        mid = rest.astype(BF16)
        low = (rest - mid.astype(F32)).astype(BF16)
        parts = _dot(tri, jnp.concatenate([hi, mid, low], axis=1))
        bcum = parts[:, :dk] + parts[:, dk:2 * dk] + parts[:, 2 * dk:]
        eb = jnp.exp(bcum)
        k_inv = k_ref[r, :].astype(F32) * (1.0 / eb)
        qd_ref[r, :] = (q_ref[r, :].astype(F32) * (eb * scale)).astype(BF16)
        ki_ref[r, :] = k_inv.astype(BF16)
        k_dec = []
        for half in range(2):
            lo, up = half * c, (half + 1) * c
            d_last = eb[up - 1:up, :]
            dl_ref[p, half] = jnp.broadcast_to(d_last, (8, dk))
            k_dec.append((k_inv[lo:up] * d_last).astype(BF16))
        kd_ref[r, :] = jnp.concatenate(k_dec, axis=0)
        return carry

    lax.fori_loop(0, n_pairs, decays, 0, unroll=8)

    def rows(slot, lane):
        return pl.ds(pl.multiple_of((slot * GLA_LANES + lane) * c2, c2), c2)

    def prepare(slot, lane):
        r = rows(slot, lane)
        a = _dot_nt(qd_ref[r, :], ki_ref[r, :])
        a_refs[lane][slot] = jnp.where(causal, a, 0.0).astype(BF16)
        k_dec = kd_ref[r, :]
        vc = v_ref[r, :]
        for half in range(2):
            lo, up = half * c, (half + 1) * c
            kv_refs[lane][slot, half] = _dot_tn(k_dec[lo:up], vc[lo:up])
            d_last = dl_ref[slot * GLA_LANES + lane, half][0:1, :]
            dc_refs[lane][slot, half] = jnp.broadcast_to(d_last, (dk, dk)).T

    def scan(slot, lane, state):
        for half in range(2):
            sb_refs[lane][slot, half] = state.astype(BF16)
            dcol = dc_refs[lane][slot, half]
            state = jnp.concatenate([dcol, dcol], axis=1) * state + kv_refs[lane][slot, half]
        return state

    def finish(slot, lane):
        r = rows(slot, lane)
        q_dec = qd_ref[r, :]
        intra = _dot(a_refs[lane][slot], v_ref[r, :])
        o = jnp.concatenate(
            [intra[h * c:(h + 1) * c] + _dot(q_dec[h * c:(h + 1) * c], sb_refs[lane][slot, h]) for h in range(2)],
            axis=0)
        y = (o * lax.rsqrt(jnp.mean(o * o, axis=-1, keepdims=True) + NORM_EPS)).astype(BF16)
        o_ref[r, :] = (y * nw) * _silu_bf16(g_ref[r, :])

    _pipeline3(n_pairs, prepare, scan, finish, jnp.zeros((dk, GLA_DV), F32), GLA_LANES)


def _gla(proj, ga, w2p, b_alpha, norm_w, batch, seq):
    m = proj.shape[0]
    h = GLA_HEADS
    n_pairs = seq // (2 * GLA_CHUNK)
    slots = n_pairs // GLA_LANES
    kern = functools.partial(_gla_kernel, n_pairs=n_pairs)
    return pl.pallas_call(
        kern,
        grid=(batch, h),
        in_specs=[
            pl.BlockSpec((seq, LANES), lambda b, hh: (b, 0)),
            pl.BlockSpec((LANES, GLA_DK), lambda b, hh: (0, hh)),
            pl.BlockSpec((1, GLA_DK), lambda b, hh: (0, hh)),
            pl.BlockSpec((seq, GLA_DK), lambda b, hh: (b, P_GQ // GLA_DK + hh)),
            pl.BlockSpec((seq, GLA_DK), lambda b, hh: (b, P_GK // GLA_DK + hh)),
            pl.BlockSpec((seq, GLA_DV), lambda b, hh: (b, P_GV // GLA_DV + hh)),
            pl.BlockSpec((seq, GLA_DV), lambda b, hh: (b, P_GG // GLA_DV + hh)),
            pl.BlockSpec((1, GLA_DV), lambda b, hh: (0, hh)),
        ],
        out_specs=pl.BlockSpec((seq, GLA_DV), lambda b, hh: (b, hh)),
        out_shape=jax.ShapeDtypeStruct((m, h * GLA_DV), BF16),
        scratch_shapes=[
            pltpu.VMEM((seq, GLA_DK), F32),
            pltpu.VMEM((seq, GLA_DK), BF16),
            pltpu.VMEM((seq, GLA_DK), BF16),
            pltpu.VMEM((seq, GLA_DK), BF16),
            pltpu.VMEM((n_pairs, 2, 8, GLA_DK), F32),
        ] + [
            pltpu.VMEM((slots, 2 * GLA_CHUNK, 2 * GLA_CHUNK), BF16),
            pltpu.VMEM((slots, 2, GLA_DK, GLA_DV), F32),
            pltpu.VMEM((slots, 2, GLA_DK, GLA_DK), F32),
            pltpu.VMEM((slots, 2, GLA_DK, GLA_DV), BF16),
        ] * GLA_LANES,
        compiler_params=_cparams("arbitrary", "arbitrary"),
        name="gla",
    )(ga, w2p, b_alpha, proj, proj, proj, proj, norm_w)


def _swa_slope(hq):
    return 2.0 ** (-8.0 * (hq + 1) / SWA_Q_HEADS)


SWA_GROUP = SWA_Q_HEADS // SWA_KV_HEADS
SWA_PAIRS = SWA_KV_HEADS // 2


def _swa_head(pair, half, g):
    return (2 * pair + half) * SWA_GROUP + g


def _swa_head_order():
    return [_swa_head(pair, half, g) for pair in range(SWA_PAIRS) for g in range(SWA_GROUP) for half in range(2)]


def _swa_kernel(sink_ref, q_ref, k_ref, v_ref, o_ref, bias_ref, lg_ref, acc_ref, es_ref, *, n_blocks):
    w = SWA_WINDOW
    dh = SWA_HEAD_DIM
    gw = SWA_GROUP * w
    nkv = SWA_KV_HEADS

    @pl.when(pl.program_id(0) == 0)
    def _():
        kpos = lax.broadcasted_iota(jnp.int32, (2 * w, w), 0)
        qpos = lax.broadcasted_iota(jnp.int32, (2 * w, w), 1) + w
        dist = qpos - kpos
        band = (dist >= 0) & (dist < w)
        distf = dist.astype(F32)
        for pair in range(SWA_PAIRS):
            for half in range(2):
                for g in range(SWA_GROUP):
                    alibi = -_swa_slope(_swa_head(pair, half, g)) * distf
                    kv = 2 * pair + half
                    bias_ref[kv, :, g * w:(g + 1) * w] = jnp.where(band, alibi, -jnp.inf)
                    bias_ref[nkv + kv, :, g * w:(g + 1) * w] = jnp.where(band & (kpos >= w), alibi, -jnp.inf)

    lane = lax.broadcasted_iota(jnp.int32, (1, 2 * dh), 1)
    in_half = [lane < dh, lane >= dh]
    sub = lax.broadcasted_iota(jnp.int32, (2 * dh, 1), 0)
    sinks = [jnp.concatenate([jnp.full((1, w), sink_ref[_swa_head(kv // 2, kv % 2, g)], F32)
                              for g in range(SWA_GROUP)], axis=1) for kv in range(nkv)]

    def block_rows(n):
        return pl.ds(n * w if isinstance(n, int) else pl.multiple_of(n * w, w), w)

    q_rows = block_rows

    def kv_block(ref, n, cols):
        prev = max(n - 1, 0) if isinstance(n, int) else n - 1
        return jnp.concatenate([ref[block_rows(prev), cols], ref[block_rows(n), cols]], axis=0)

    def scores(n, par):
        table = nkv if isinstance(n, int) and n == 0 else 0
        for pair in range(SWA_PAIRS):
            qst = jnp.concatenate(
                [q_ref[q_rows(n), (pair * SWA_GROUP + g) * 2 * dh:(pair * SWA_GROUP + g + 1) * 2 * dh]
                 for g in range(SWA_GROUP)], axis=0)
            kp = kv_block(k_ref, n, slice(pair * 2 * dh, (pair + 1) * 2 * dh)) * (dh ** -0.5)
            for half in range(2):
                kv = 2 * pair + half
                kh = jnp.where(in_half[half], kp, jnp.zeros_like(kp))
                lg_ref[par, kv] = _dot_nt(kh, qst) + bias_ref[table + kv]

    def probs(n, par):
        for pair in range(SWA_PAIRS):
            vp = kv_block(v_ref, n, slice(pair * 2 * dh, (pair + 1) * 2 * dh))
            for half in range(2):
                kv = 2 * pair + half
                logits = lg_ref[par, kv]
                mx = jnp.maximum(jnp.max(logits, axis=0, keepdims=True), sinks[kv])
                p = jnp.exp(logits - mx).astype(BF16)
                vh = jnp.where(in_half[half], vp, jnp.ones_like(vp))
                acc_ref[par, kv] = _dot_tn(vh, p)
                es_ref[par, kv] = jnp.broadcast_to(jnp.exp(sinks[kv] - mx), (8, gw))

    def outputs(n, par):
        for pair in range(SWA_PAIRS):
            out_t = None
            for half in range(2):
                kv = 2 * pair + half
                acc = acc_ref[par, kv]
                other = (1 - half) * dh
                den = acc[other:other + 1, :] + es_ref[par, kv][0:1, :]
                acc = acc * (1.0 / den)
                out_t = acc if out_t is None else jnp.where(sub < dh, out_t, acc)
            for g in range(SWA_GROUP):
                blk = pair * SWA_GROUP + g
                o_ref[q_rows(n), blk * 2 * dh:(blk + 1) * 2 * dh] = out_t[:, g * w:(g + 1) * w].T.astype(BF16)

    _pipeline3_ring(n_blocks, scores, probs, outputs)


def _swa(proj, sinks, batch, seq):
    m = proj.shape[0]
    w = SWA_WINDOW
    qw = SWA_Q_HEADS * SWA_HEAD_DIM
    kw = SWA_KV_HEADS * SWA_HEAD_DIM
    kern = functools.partial(_swa_kernel, n_blocks=seq // w)
    return pl.pallas_call(
        kern,
        grid=(batch,),
        in_specs=[
            pl.BlockSpec(memory_space=pltpu.SMEM),
            pl.BlockSpec((seq, qw), lambda b: (b, P_SQ // qw)),
            pl.BlockSpec((seq, kw), lambda b: (b, P_SK // kw)),
            pl.BlockSpec((seq, kw), lambda b: (b, P_SV // kw)),
        ],
        out_specs=pl.BlockSpec((seq, qw), lambda b: (b, 0)),
        out_shape=jax.ShapeDtypeStruct((m, qw), BF16),
        scratch_shapes=[
            pltpu.VMEM((2 * SWA_KV_HEADS, 2 * w, SWA_GROUP * w), F32),
            pltpu.VMEM((2, SWA_KV_HEADS, 2 * w, SWA_GROUP * w), F32),
            pltpu.VMEM((2, SWA_KV_HEADS, 2 * SWA_HEAD_DIM, SWA_GROUP * w), F32),
            pltpu.VMEM((2, SWA_KV_HEADS, 8, SWA_GROUP * w), F32),
        ],
        compiler_params=_cparams("arbitrary"),
        name="swa",
    )(sinks, proj, proj, proj)


def _merge_kernel(x_ref, gl_ref, r_ref, g_ref, s_ref, wb_ref, wo_ref, nw_ref, o_ref):
    d = D_MODEL
    merged = None
    for n, br in enumerate((r_ref, g_ref, s_ref)):
        p = _dot(br[...], wb_ref[n])
        gate = jax.nn.sigmoid(gl_ref[:, n * d:(n + 1) * d].astype(F32))
        merged = gate * p if merged is None else merged + gate * p
    y = _dot(merged.astype(BF16), wo_ref[...])
    o_ref[...] = x_ref[...] + _rms(y, nw_ref[...], NORM_EPS)


def _merge(x2, proj, o_ret, o_gla, o_swa, wb, wo, nw, layer, tm):
    m, d = x2.shape
    const = pl.Buffered(1)
    return pl.pallas_call(
        _merge_kernel,
        grid=(m // tm,),
        in_specs=[
            pl.BlockSpec((tm, d), lambda i: (i, 0)),
            pl.BlockSpec((tm, N_BRANCH * d), lambda i: (i, P_GATE // (N_BRANCH * d))),
            pl.BlockSpec((tm, d), lambda i: (i, 0)),
            pl.BlockSpec((tm, d), lambda i: (i, 0)),
            pl.BlockSpec((tm, d), lambda i: (i, 0)),
            pl.BlockSpec((None, N_BRANCH, d, d), lambda i: (layer, 0, 0, 0), pipeline_mode=const),
            pl.BlockSpec((None, d, d), lambda i: (layer, 0, 0), pipeline_mode=const),
            pl.BlockSpec((1, d), lambda i: (0, 0)),
        ],
        out_specs=pl.BlockSpec((tm, d), lambda i: (i, 0)),
        out_shape=jax.ShapeDtypeStruct((m, d), F32),
        compiler_params=_cparams("arbitrary"),
        name="merge",
    )(x2, proj, o_ret, o_gla, o_swa, wb, wo, nw)


def _ffn_kernel(x_ref, wg_ref, wu_ref, wd_ref, npre_ref, npost_ref, o_ref):
    x = x_ref[...]
    h = _rms(x, npre_ref[...], NORM_EPS).astype(BF16)
    gate = _dot(h, wg_ref[...])
    up = _dot(h, wu_ref[...])
    act = (gate * jax.nn.sigmoid(gate) * up).astype(BF16)
    f = _dot(act, wd_ref[...])
    o_ref[...] = x + _rms(f, npost_ref[...], NORM_EPS)


def _ffn(x2, wg, wu, wd, npre, npost, layer, tm):
    m, d = x2.shape
    ff = wg.shape[2]
    const = pl.Buffered(1)
    return pl.pallas_call(
        _ffn_kernel,
        grid=(m // tm,),
        in_specs=[
            pl.BlockSpec((tm, d), lambda i: (i, 0)),
            pl.BlockSpec((None, d, ff), lambda i: (layer, 0, 0), pipeline_mode=const),
            pl.BlockSpec((None, d, ff), lambda i: (layer, 0, 0), pipeline_mode=const),
            pl.BlockSpec((None, ff, d), lambda i: (layer, 0, 0), pipeline_mode=const),
            pl.BlockSpec((1, d), lambda i: (0, 0)),
            pl.BlockSpec((1, d), lambda i: (0, 0)),
        ],
        out_specs=pl.BlockSpec((tm, d), lambda i: (i, 0)),
        out_shape=jax.ShapeDtypeStruct((m, d), F32),
        compiler_params=_cparams("arbitrary"),
        name="ffn",
    )(x2, wg, wu, wd, npre, npost)


def _tile(m, want):
    t = min(want, m)
    assert m % t == 0
    return t


def kernel(x, norm_mix_pre, norm_mix_post, w_in, ret_norm_w, ret_norm_b, gla_w_alpha2, gla_b_alpha,
           gla_norm_w, attn_sinks, w_branch, w_out, norm_ffn_pre, norm_ffn_post, ffn_w_gate, ffn_w_up,
           ffn_w_down):
    batch, seq, d = x.shape
    assert d == D_MODEL and seq % SWA_WINDOW == 0
    m = batch * seq
    x2 = x.reshape(m, d)
    log_g = jnp.log1p(-jnp.exp2(-5.0 - jnp.arange(RET_HEADS, dtype=F32)))
    log_g = jnp.repeat(log_g, LANES)[None, :]

    layers = w_in.shape[0]

    def swa_order(t, axis):
        shape = t.shape
        t = t.reshape(shape[:axis] + (SWA_PAIRS, 2, SWA_GROUP, SWA_HEAD_DIM) + shape[axis + 1:])
        t = jnp.swapaxes(t, axis + 1, axis + 2)
        return t.reshape(shape)

    o_sk = _O_SWA + SWA_Q_HEADS * SWA_HEAD_DIM
    w_main = jnp.concatenate(
        [w_in[:, :, _O_GATE:_O_END], w_in[:, :, _O_RET:_O_GA], swa_order(w_in[:, :, _O_SWA:o_sk], 2),
         w_in[:, :, o_sk:_O_GATE]], axis=2).astype(BF16)
    w_ga = jnp.pad(w_in[:, :, _O_GA:_O_SWA], ((0, 0), (0, 0), (0, LANES - GLA_RANK))).astype(BF16)
    w2p = jnp.pad(gla_w_alpha2, ((0, 0), (0, LANES - GLA_RANK), (0, 0))).astype(BF16)
    w_br = jnp.concatenate([w_branch[:, :2], swa_order(w_branch[:, 2], 1)[:, None]], axis=1).astype(BF16)
    w_o = w_out.astype(BF16)
    w_fg, w_fu, w_fd = ffn_w_gate.astype(BF16), ffn_w_up.astype(BF16), ffn_w_down.astype(BF16)
    assert w_main.shape == (layers, d, P_TOTAL)

    for l in range(DEPTH):
        proj, ga = _inproj(x2, norm_mix_pre[l][None, :], w_main, w_ga, l, _tile(m, 1024), 1792)
        o_ret = _retention(proj, log_g, ret_norm_w[l][None, :], ret_norm_b[l][None, :], batch, seq)
        o_gla = _gla(proj, ga, w2p[l], gla_b_alpha[l][None, :], gla_norm_w[l][None, :], batch, seq)
        o_swa = _swa(proj, attn_sinks[l], batch, seq)
        x2 = _merge(x2, proj, o_ret, o_gla, o_swa, w_br, w_o, norm_mix_post[l][None, :], l, _tile(m, 512))
        x2 = _ffn(x2, w_fg, w_fu, w_fd, norm_ffn_pre[l][None, :], norm_ffn_post[l][None, :], l, _tile(m, 512))
    return x2.reshape(batch, seq, d)
```

```python
import functools

import jax
import jax.numpy as jnp
from jax import lax
from jax.experimental import pallas as pl
from jax.experimental.pallas import tpu as pltpu

F32 = jnp.float32
BF16 = jnp.bfloat16

D_MODEL = 1024
DEPTH = 2
RET_HEADS, RET_DK, RET_DV, RET_CHUNK = 4, 128, 256, 128
GLA_HEADS, GLA_DK, GLA_DV, GLA_RANK, GLA_CHUNK = 4, 128, 256, 16, 64
GLA_TAU, GLA_LOG_GATE_MIN = 16.0, -1.0
SWA_Q_HEADS, SWA_KV_HEADS, SWA_HEAD_DIM, SWA_WINDOW = 16, 4, 64, 128
N_BRANCH = 3
D_FF = 2816
NORM_EPS = 1e-6
GN_EPS = 1e-5

LANES = 128
VMEM_LIMIT_BYTES = 56 * 1024 * 1024

_O_RET, _O_GLA, _O_GA, _O_SWA, _O_GATE, _O_END = 0, 3072, 6144, 6160, 7696, 10768
P_GATE = 0
P_RQ, P_RK, P_RV, P_RG = 3072, 3584, 4096, 5120
P_GQ, P_GK, P_GV, P_GG = 6144, 6656, 7168, 8192
P_SQ, P_SK, P_SV = 9216, 10240, 10496
P_TOTAL = 10752


def _cparams(*sem):
    return pltpu.CompilerParams(dimension_semantics=sem, vmem_limit_bytes=VMEM_LIMIT_BYTES)


def _rms(x, w, eps):
    return x * lax.rsqrt(jnp.mean(x * x, axis=-1, keepdims=True) + eps) * w


def _dot(a, b):
    return jnp.dot(a, b, preferred_element_type=F32)


def _dot_nt(a, b):
    return lax.dot_general(a, b, (((1,), (1,)), ((), ())), preferred_element_type=F32)


def _silu_bf16(g):
    h = g * 0.5
    return h + h * jnp.tanh(h)


def _dot_tn(a, b):
    return lax.dot_general(a, b, (((0,), (0,)), ((), ())), preferred_element_type=F32)


def _pipeline3(n, first, middle, last, state, lanes):
    assert n % lanes == 0
    rounds = n // lanes

    def run_round(r, state, do_last, do_middle, do_first):
        for lane in range(lanes):
            if do_last:
                last(r - 2, lane)
            if do_middle:
                state = middle(r - 1, lane, state)
            if do_first:
                first(r, lane)
        return state

    for r in range(min(2, rounds + 2)):
        state = run_round(r, state, False, 0 <= r - 1 < rounds, r < rounds)
    if rounds > 2:
        state = lax.fori_loop(2, rounds, lambda r, s: run_round(r, s, True, True, True), state)
    for r in range(max(rounds, 2), rounds + 2):
        state = run_round(r, state, r - 2 < rounds, r - 1 < rounds, False)
    return state


def _pipeline3_ring(n, first, middle, last):
    def run_round(r, par):
        static = isinstance(r, int)
        if not static or r < n:
            first(r, par)
        if not static or 0 <= r - 1 < n:
            middle(r - 1, 1 - par)
        if not static or 0 <= r - 2 < n:
            last(r - 2, par)

    lo = 2
    hi = max(lo, n - (n - lo) % 2)
    for r in range(min(lo, n + 2)):
        run_round(r, r % 2)
    if hi > lo:
        def two_rounds(j, carry):
            r = lo + 2 * j
            run_round(r, 0)
            run_round(r + 1, 1)
            return carry

        lax.fori_loop(0, (hi - lo) // 2, two_rounds, 0)
    for r in range(hi, n + 2):
        run_round(r, r % 2)


def _inproj_kernel(x_ref, g_ref, w_ref, wga_ref, o_ref, oga_ref, h_ref):
    @pl.when(pl.program_id(1) == 0)
    def _():
        h = _rms(x_ref[...], g_ref[...], NORM_EPS).astype(BF16)
        h_ref[...] = h
        oga_ref[...] = _dot(h, wga_ref[...]).astype(BF16)

    o_ref[...] = _dot(h_ref[...], w_ref[...]).astype(BF16)


def _inproj(x2, gain, w_main, w_ga, layer, tm, tn):
    m, d = x2.shape
    n = w_main.shape[2]
    return pl.pallas_call(
        _inproj_kernel,
        grid=(m // tm, n // tn),
        in_specs=[
            pl.BlockSpec((tm, d), lambda i, j: (i, 0)),
            pl.BlockSpec((1, d), lambda i, j: (0, 0)),
            pl.BlockSpec((None, d, tn), lambda i, j: (layer, 0, j)),
            pl.BlockSpec((None, d, LANES), lambda i, j: (layer, 0, 0)),
        ],
        out_specs=[
            pl.BlockSpec((tm, tn), lambda i, j: (i, j)),
            pl.BlockSpec((tm, LANES), lambda i, j: (i, 0)),
        ],
        out_shape=[jax.ShapeDtypeStruct((m, n), BF16), jax.ShapeDtypeStruct((m, LANES), BF16)],
        scratch_shapes=[pltpu.VMEM((tm, d), BF16)],
        compiler_params=_cparams("arbitrary", "arbitrary"),
        name="inproj",
    )(x2, gain, w_main, w_ga)


def _inproj_normed_kernel(h_ref, w_ref, wga_ref, o_ref, oga_ref):
    @pl.when(pl.program_id(1) == 0)
    def _():
        oga_ref[...] = _dot(h_ref[...], wga_ref[...]).astype(BF16)

    o_ref[...] = _dot(h_ref[...], w_ref[...]).astype(BF16)


def _inproj_normed(h, w_main, w_ga, layer, tm, tn):
    m, d = h.shape
    n = w_main.shape[2]
    return pl.pallas_call(
        _inproj_normed_kernel,
        grid=(m // tm, n // tn),
        in_specs=[
            pl.BlockSpec((tm, d), lambda i, j: (i, 0)),
            pl.BlockSpec((None, d, tn), lambda i, j: (layer, 0, j)),
            pl.BlockSpec((None, d, LANES), lambda i, j: (layer, 0, 0)),
        ],
        out_specs=[
            pl.BlockSpec((tm, tn), lambda i, j: (i, j)),
            pl.BlockSpec((tm, LANES), lambda i, j: (i, 0)),
        ],
        out_shape=[jax.ShapeDtypeStruct((m, n), BF16), jax.ShapeDtypeStruct((m, LANES), BF16)],
        compiler_params=_cparams("arbitrary", "arbitrary"),
        name="inproj_normed",
    )(h, w_main, w_ga)


RET_LANES = 8


def _ret_kernel(lg_ref, q_ref, k_ref, v_ref, g_ref, nw_ref, nb_ref, o_ref, *scratch, n_chunks, heads):
    c = RET_CHUNK
    row = lax.broadcasted_iota(jnp.int32, (c, c), 0)
    col = lax.broadcasted_iota(jnp.int32, (c, c), 1)
    rel = (row - col).astype(F32)
    rowf = row.astype(F32)
    scale = RET_DK ** -0.5
    lhs_refs, kv_refs, acc_refs = scratch[0::3], scratch[1::3], scratch[2::3]

    def rows(slot, lane):
        return pl.ds(pl.multiple_of((slot * RET_LANES + lane) * c, c), c)

    def one_head(h, carry):
        kcols = pl.ds(pl.multiple_of(h * RET_DK, RET_DK), RET_DK)
        vcols = pl.ds(pl.multiple_of(h * RET_DV, RET_DV), RET_DV)
        lg = lg_ref[:, kcols]
        decay = jnp.where(row >= col, jnp.exp(rel * lg), 0.0) * scale
        xi = (jnp.exp((rowf + 1.0) * lg) * scale).astype(BF16)
        zeta = jnp.exp((c - 1.0 - rowf) * lg).astype(BF16)
        gamma_c = jnp.exp(c * lg)
        gamma_c = jnp.concatenate([gamma_c, gamma_c], axis=1)
        nw = nw_ref[:, vcols].astype(BF16)
        nb = nb_ref[:, vcols].astype(BF16)

        def scores(slot, lane):
            r = rows(slot, lane)
            qc = q_ref[r, kcols]
            kc = k_ref[r, kcols]
            s = _dot_nt(qc, kc) * decay
            lhs_refs[lane][slot] = jnp.concatenate([s.astype(BF16), qc * xi], axis=1)
            kv_refs[lane][slot] = _dot_tn(kc * zeta, v_ref[r, vcols])

        def mix(slot, lane, state):
            rhs = jnp.concatenate([v_ref[rows(slot, lane), vcols], state.astype(BF16)], axis=0)
            acc_refs[lane][slot] = _dot(lhs_refs[lane][slot], rhs)
            return gamma_c * state + kv_refs[lane][slot]

        def finish(slot, lane):
            r = rows(slot, lane)
            o = acc_refs[lane][slot]
            mu = jnp.mean(o, axis=-1, keepdims=True)
            d = o - mu
            var = jnp.mean(d * d, axis=-1, keepdims=True)
            y = (d * lax.rsqrt(var + GN_EPS)).astype(BF16)
            o_ref[r, vcols] = (y * nw + nb) * _silu_bf16(g_ref[r, vcols])

        _pipeline3(n_chunks, scores, mix, finish, jnp.zeros((RET_DK, RET_DV), F32), RET_LANES)
        return carry

    lax.fori_loop(0, heads, one_head, 0)


def _retention(proj, log_g, norm_w, norm_b, batch, seq):
    m = proj.shape[0]
    h = RET_HEADS
    n_chunks = seq // RET_CHUNK
    kern = functools.partial(_ret_kernel, n_chunks=n_chunks, heads=h)
    return pl.pallas_call(
        kern,
        grid=(batch,),
        in_specs=[
            pl.BlockSpec((1, h * LANES), lambda b: (0, 0)),
            pl.BlockSpec((seq, h * RET_DK), lambda b: (b, P_RQ // (h * RET_DK))),
            pl.BlockSpec((seq, h * RET_DK), lambda b: (b, P_RK // (h * RET_DK))),
            pl.BlockSpec((seq, h * RET_DV), lambda b: (b, P_RV // (h * RET_DV))),
            pl.BlockSpec((seq, h * RET_DV), lambda b: (b, P_RG // (h * RET_DV))),
            pl.BlockSpec((1, h * RET_DV), lambda b: (0, 0)),
            pl.BlockSpec((1, h * RET_DV), lambda b: (0, 0)),
        ],
        out_specs=pl.BlockSpec((seq, h * RET_DV), lambda b: (b, 0)),
        out_shape=jax.ShapeDtypeStruct((m, h * RET_DV), BF16),
        scratch_shapes=[
            pltpu.VMEM((n_chunks // RET_LANES, RET_CHUNK, RET_CHUNK + RET_DK), BF16),
            pltpu.VMEM((n_chunks // RET_LANES, RET_DK, RET_DV), F32),
            pltpu.VMEM((n_chunks // RET_LANES, RET_CHUNK, RET_DV), F32),
        ] * RET_LANES,
        compiler_params=_cparams("arbitrary"),
        name="retention",
    )(log_g, proj, proj, proj, proj, norm_w, norm_b)


GLA_LANES = 8


def _gla_kernel(ga_ref, w2_ref, ba_ref, q_ref, k_ref, v_ref, g_ref, nw_ref, o_ref, la_ref, *scratch, n_pairs, heads):
    c = GLA_CHUNK
    c2 = 2 * c
    dk = GLA_DK
    z = _dot(ga_ref[...], w2_ref[...]) + ba_ref[...]
    log_sig = jnp.minimum(z, 0.0) - jnp.log(1.0 + jnp.exp(-jnp.abs(z)))
    la_ref[...] = jnp.maximum(log_sig * (1.0 / GLA_TAU), GLA_LOG_GATE_MIN)

    row = lax.broadcasted_iota(jnp.int32, (c2, c2), 0)
    col = lax.broadcasted_iota(jnp.int32, (c2, c2), 1)
    causal = (row >= col) & ((row >= c) == (col >= c))
    tri = jnp.where(causal, 1.0, 0.0).astype(BF16)
    scale = dk ** -0.5

    qd_ref, ki_ref, kd_ref, dl_ref = scratch[:4]
    a_refs, kv_refs, dc_refs, sb_refs = (scratch[4 + j::4] for j in range(4))

    def rows(slot, lane):
        return pl.ds(pl.multiple_of((slot * GLA_LANES + lane) * c2, c2), c2)

    def one_head(h, carry):
        kcols = pl.ds(pl.multiple_of(h * dk, dk), dk)
        vcols = pl.ds(pl.multiple_of(h * GLA_DV, GLA_DV), GLA_DV)
        nw = nw_ref[:, vcols].astype(BF16)

        def decays(p, carry):
            r = pl.ds(pl.multiple_of(p * c2, c2), c2)
            la = la_ref[r, kcols]
            hi = la.astype(BF16)
            rest = la - hi.astype(F32)
            mid = rest.astype(BF16)
            low = (rest - mid.astype(F32)).astype(BF16)
            parts = _dot(tri, jnp.concatenate([hi, mid, low], axis=1))
            bcum = parts[:, :dk] + parts[:, dk:2 * dk] + parts[:, 2 * dk:]
            eb = jnp.exp(bcum)
            k_inv = k_ref[r, kcols].astype(F32) * (1.0 / eb)
            qd_ref[r, :] = (q_ref[r, kcols].astype(F32) * (eb * scale)).astype(BF16)
            ki_ref[r, :] = k_inv.astype(BF16)
            k_dec = []
            for half in range(2):
                lo, up = half * c, (half + 1) * c
                d_last = eb[up - 1:up, :]
                dl_ref[p, half] = jnp.broadcast_to(d_last, (8, dk))
                k_dec.append((k_inv[lo:up] * d_last).astype(BF16))
            kd_ref[r, :] = jnp.concatenate(k_dec, axis=0)
            return carry

        lax.fori_loop(0, n_pairs, decays, 0, unroll=8)

        def prepare(slot, lane):
            r = rows(slot, lane)
            a = _dot_nt(qd_ref[r, :], ki_ref[r, :])
            a_refs[lane][slot] = jnp.where(causal, a, 0.0).astype(BF16)
            k_dec = kd_ref[r, :]
            vc = v_ref[r, vcols]
            for half in range(2):
                lo, up = half * c, (half + 1) * c
                kv_refs[lane][slot, half] = _dot_tn(k_dec[lo:up], vc[lo:up])
                d_last = dl_ref[slot * GLA_LANES + lane, half][0:1, :]
                dc_refs[lane][slot, half] = jnp.broadcast_to(d_last, (dk, dk)).T

        def scan(slot, lane, state):
            for half in range(2):
                sb_refs[lane][slot, half] = state.astype(BF16)
                dcol = dc_refs[lane][slot, half]
                state = jnp.concatenate([dcol, dcol], axis=1) * state + kv_refs[lane][slot, half]
            return state

        def finish(slot, lane):
            r = rows(slot, lane)
            q_dec = qd_ref[r, :]
            intra = _dot(a_refs[lane][slot], v_ref[r, vcols])
            o = jnp.concatenate(
                [intra[i * c:(i + 1) * c] + _dot(q_dec[i * c:(i + 1) * c], sb_refs[lane][slot, i])
                 for i in range(2)], axis=0)
            y = (o * lax.rsqrt(jnp.mean(o * o, axis=-1, keepdims=True) + NORM_EPS)).astype(BF16)
            o_ref[r, vcols] = (y * nw) * _silu_bf16(g_ref[r, vcols])

        _pipeline3(n_pairs, prepare, scan, finish, jnp.zeros((dk, GLA_DV), F32), GLA_LANES)
        return carry

    lax.fori_loop(0, heads, one_head, 0)


def _gla(proj, ga, w2p, b_alpha, norm_w, batch, seq):
    m = proj.shape[0]
    h = GLA_HEADS
    n_pairs = seq // (2 * GLA_CHUNK)
    slots = n_pairs // GLA_LANES
    kern = functools.partial(_gla_kernel, n_pairs=n_pairs, heads=h)
    return pl.pallas_call(
        kern,
        grid=(batch,),
        in_specs=[
            pl.BlockSpec((seq, LANES), lambda b: (b, 0)),
            pl.BlockSpec((LANES, h * GLA_DK), lambda b: (0, 0)),
            pl.BlockSpec((1, h * GLA_DK), lambda b: (0, 0)),
            pl.BlockSpec((seq, h * GLA_DK), lambda b: (b, P_GQ // (h * GLA_DK))),
            pl.BlockSpec((seq, h * GLA_DK), lambda b: (b, P_GK // (h * GLA_DK))),
            pl.BlockSpec((seq, h * GLA_DV), lambda b: (b, P_GV // (h * GLA_DV))),
            pl.BlockSpec((seq, h * GLA_DV), lambda b: (b, P_GG // (h * GLA_DV))),
            pl.BlockSpec((1, h * GLA_DV), lambda b: (0, 0)),
        ],
        out_specs=pl.BlockSpec((seq, h * GLA_DV), lambda b: (b, 0)),
        out_shape=jax.ShapeDtypeStruct((m, h * GLA_DV), BF16),
        scratch_shapes=[
            pltpu.VMEM((seq, h * GLA_DK), F32),
            pltpu.VMEM((seq, GLA_DK), BF16),
            pltpu.VMEM((seq, GLA_DK), BF16),
            pltpu.VMEM((seq, GLA_DK), BF16),
            pltpu.VMEM((n_pairs, 2, 8, GLA_DK), F32),
        ] + [
            pltpu.VMEM((slots, 2 * GLA_CHUNK, 2 * GLA_CHUNK), BF16),
            pltpu.VMEM((slots, 2, GLA_DK, GLA_DV), F32),
            pltpu.VMEM((slots, 2, GLA_DK, GLA_DK), F32),
            pltpu.VMEM((slots, 2, GLA_DK, GLA_DV), BF16),
        ] * GLA_LANES,
        compiler_params=_cparams("arbitrary"),
        name="gla",
    )(ga, w2p, b_alpha, proj, proj, proj, proj, norm_w)


def _swa_slope(hq):
    return 2.0 ** (-8.0 * (hq + 1) / SWA_Q_HEADS)


SWA_GROUP = SWA_Q_HEADS // SWA_KV_HEADS
SWA_PAIRS = SWA_KV_HEADS // 2


def _swa_head(pair, half, g):
    return (2 * pair + half) * SWA_GROUP + g


def _swa_head_order():
    return [_swa_head(pair, half, g) for pair in range(SWA_PAIRS) for g in range(SWA_GROUP) for half in range(2)]


def _swa_kernel(sink_ref, q_ref, k_ref, v_ref, o_ref, bias_ref, lg_ref, acc_ref, es_ref, *, n_blocks):
    w = SWA_WINDOW
    dh = SWA_HEAD_DIM
    gw = SWA_GROUP * w
    nkv = SWA_KV_HEADS

    @pl.when(pl.program_id(0) == 0)
    def _():
        kpos = lax.broadcasted_iota(jnp.int32, (2 * w, w), 0)
        qpos = lax.broadcasted_iota(jnp.int32, (2 * w, w), 1) + w
        dist = qpos - kpos
        band = (dist >= 0) & (dist < w)
        distf = dist.astype(F32)
        for pair in range(SWA_PAIRS):
            for half in range(2):
                for g in range(SWA_GROUP):
                    alibi = -_swa_slope(_swa_head(pair, half, g)) * distf
                    kv = 2 * pair + half
                    bias_ref[kv, :, g * w:(g + 1) * w] = jnp.where(band, alibi, -jnp.inf)
                    bias_ref[nkv + kv, :, g * w:(g + 1) * w] = jnp.where(band & (kpos >= w), alibi, -jnp.inf)

    lane = lax.broadcasted_iota(jnp.int32, (1, 2 * dh), 1)
    in_half = [lane < dh, lane >= dh]
    sub = lax.broadcasted_iota(jnp.int32, (2 * dh, 1), 0)
    sinks = [jnp.concatenate([jnp.full((1, w), sink_ref[_swa_head(kv // 2, kv % 2, g)], F32)
                              for g in range(SWA_GROUP)], axis=1) for kv in range(nkv)]

    def block_rows(n):
        return pl.ds(n * w if isinstance(n, int) else pl.multiple_of(n * w, w), w)

    q_rows = block_rows

    def kv_block(ref, n, cols):
        prev = max(n - 1, 0) if isinstance(n, int) else n - 1
        return jnp.concatenate([ref[block_rows(prev), cols], ref[block_rows(n), cols]], axis=0)

    def scores(n, par):
        table = nkv if isinstance(n, int) and n == 0 else 0
        for pair in range(SWA_PAIRS):
            qst = jnp.concatenate(
                [q_ref[q_rows(n), (pair * SWA_GROUP + g) * 2 * dh:(pair * SWA_GROUP + g + 1) * 2 * dh]
                 for g in range(SWA_GROUP)], axis=0)
            kp = kv_block(k_ref, n, slice(pair * 2 * dh, (pair + 1) * 2 * dh)) * (dh ** -0.5)
            for half in range(2):
                kv = 2 * pair + half
                kh = jnp.where(in_half[half], kp, jnp.zeros_like(kp))
                lg_ref[par, kv] = _dot_nt(kh, qst) + bias_ref[table + kv]

    def probs(n, par):
        for pair in range(SWA_PAIRS):
            vp = kv_block(v_ref, n, slice(pair * 2 * dh, (pair + 1) * 2 * dh))
            for half in range(2):
                kv = 2 * pair + half
                logits = lg_ref[par, kv]
                mx = jnp.maximum(jnp.max(logits, axis=0, keepdims=True), sinks[kv])
                p = jnp.exp(logits - mx).astype(BF16)
                vh = jnp.where(in_half[half], vp, jnp.ones_like(vp))
                acc_ref[par, kv] = _dot_tn(vh, p)
                es_ref[par, kv] = jnp.broadcast_to(jnp.exp(sinks[kv] - mx), (8, gw))

    def outputs(n, par):
        for pair in range(SWA_PAIRS):
            out_t = None
            for half in range(2):
                kv = 2 * pair + half
                acc = acc_ref[par, kv]
                other = (1 - half) * dh
                den = acc[other:other + 1, :] + es_ref[par, kv][0:1, :]
                acc = acc * (1.0 / den)
                out_t = acc if out_t is None else jnp.where(sub < dh, out_t, acc)
            for g in range(SWA_GROUP):
                blk = pair * SWA_GROUP + g
                o_ref[q_rows(n), blk * 2 * dh:(blk + 1) * 2 * dh] = out_t[:, g * w:(g + 1) * w].T.astype(BF16)

    _pipeline3_ring(n_blocks, scores, probs, outputs)


def _swa(proj, sinks, batch, seq):
    m = proj.shape[0]
    w = SWA_WINDOW
    qw = SWA_Q_HEADS * SWA_HEAD_DIM
    kw = SWA_KV_HEADS * SWA_HEAD_DIM
    kern = functools.partial(_swa_kernel, n_blocks=seq // w)
    return pl.pallas_call(
        kern,
        grid=(batch,),
        in_specs=[
            pl.BlockSpec(memory_space=pltpu.SMEM),
            pl.BlockSpec((seq, qw), lambda b: (b, P_SQ // qw)),
            pl.BlockSpec((seq, kw), lambda b: (b, P_SK // kw)),
            pl.BlockSpec((seq, kw), lambda b: (b, P_SV // kw)),
        ],
        out_specs=pl.BlockSpec((seq, qw), lambda b: (b, 0)),
        out_shape=jax.ShapeDtypeStruct((m, qw), BF16),
        scratch_shapes=[
            pltpu.VMEM((2 * SWA_KV_HEADS, 2 * w, SWA_GROUP * w), F32),
            pltpu.VMEM((2, SWA_KV_HEADS, 2 * w, SWA_GROUP * w), F32),
            pltpu.VMEM((2, SWA_KV_HEADS, 2 * SWA_HEAD_DIM, SWA_GROUP * w), F32),
            pltpu.VMEM((2, SWA_KV_HEADS, 8, SWA_GROUP * w), F32),
        ],
        compiler_params=_cparams("arbitrary"),
        name="swa",
    )(sinks, proj, proj, proj)


def _merge_kernel(x_ref, gl_ref, r_ref, g_ref, s_ref, wb_ref, wo_ref, nw_ref, o_ref):
    d = D_MODEL
    merged = None
    for n, br in enumerate((r_ref, g_ref, s_ref)):
        p = _dot(br[...], wb_ref[n])
        gate = jax.nn.sigmoid(gl_ref[:, n * d:(n + 1) * d].astype(F32))
        merged = gate * p if merged is None else merged + gate * p
    y = _dot(merged.astype(BF16), wo_ref[...])
    o_ref[...] = x_ref[...] + _rms(y, nw_ref[...], NORM_EPS)


def _merge(x2, proj, o_ret, o_gla, o_swa, wb, wo, nw, layer, tm):
    m, d = x2.shape
    const = pl.Buffered(1)
    return pl.pallas_call(
        _merge_kernel,
        grid=(m // tm,),
        in_specs=[
            pl.BlockSpec((tm, d), lambda i: (i, 0)),
            pl.BlockSpec((tm, N_BRANCH * d), lambda i: (i, P_GATE // (N_BRANCH * d))),
            pl.BlockSpec((tm, d), lambda i: (i, 0)),
            pl.BlockSpec((tm, d), lambda i: (i, 0)),
            pl.BlockSpec((tm, d), lambda i: (i, 0)),
            pl.BlockSpec((None, N_BRANCH, d, d), lambda i: (layer, 0, 0, 0), pipeline_mode=const),
            pl.BlockSpec((None, d, d), lambda i: (layer, 0, 0), pipeline_mode=const),
            pl.BlockSpec((1, d), lambda i: (0, 0)),
        ],
        out_specs=pl.BlockSpec((tm, d), lambda i: (i, 0)),
        out_shape=jax.ShapeDtypeStruct((m, d), F32),
        compiler_params=_cparams("arbitrary"),
        name="merge",
    )(x2, proj, o_ret, o_gla, o_swa, wb, wo, nw)


def _ffn_kernel(x_ref, wg_ref, wu_ref, wd_ref, npre_ref, npost_ref, *rest):
    x = x_ref[...]
    h = _rms(x, npre_ref[...], NORM_EPS).astype(BF16)
    gate = _dot(h, wg_ref[...])
    up = _dot(h, wu_ref[...])
    act = (gate * jax.nn.sigmoid(gate) * up).astype(BF16)
    f = _dot(act, wd_ref[...])
    out = x + _rms(f, npost_ref[...], NORM_EPS)
    if len(rest) == 1:
        (o_ref,) = rest
    else:
        nnext_ref, o_ref, hn_ref = rest
        hn_ref[...] = _rms(out, nnext_ref[...], NORM_EPS).astype(BF16)
    o_ref[...] = out


def _ffn(x2, wg, wu, wd, npre, npost, layer, tm, next_gain=None):
    m, d = x2.shape
    ff = wg.shape[2]
    const = pl.Buffered(1)
    row_spec = pl.BlockSpec((tm, d), lambda i: (i, 0))
    vec_spec = pl.BlockSpec((1, d), lambda i: (0, 0))
    in_specs = [
        row_spec,
        pl.BlockSpec((None, d, ff), lambda i: (layer, 0, 0), pipeline_mode=const),
        pl.BlockSpec((None, d, ff), lambda i: (layer, 0, 0), pipeline_mode=const),
        pl.BlockSpec((None, ff, d), lambda i: (layer, 0, 0), pipeline_mode=const),
        vec_spec,
        vec_spec,
    ]
    args = [x2, wg, wu, wd, npre, npost]
    out_specs, out_shape = row_spec, jax.ShapeDtypeStruct((m, d), F32)
    if next_gain is not None:
        in_specs.append(vec_spec)
        args.append(next_gain)
        out_specs = [row_spec, row_spec]
        out_shape = [out_shape, jax.ShapeDtypeStruct((m, d), BF16)]
    return pl.pallas_call(
        _ffn_kernel,
        grid=(m // tm,),
        in_specs=in_specs,
        out_specs=out_specs,
        out_shape=out_shape,
        compiler_params=_cparams("arbitrary"),
        name="ffn",
    )(*args)


def _tile(m, want):
    t = min(want, m)
    assert m % t == 0
    return t


def kernel(x, norm_mix_pre, norm_mix_post, w_in, ret_norm_w, ret_norm_b, gla_w_alpha2, gla_b_alpha,
           gla_norm_w, attn_sinks, w_branch, w_out, norm_ffn_pre, norm_ffn_post, ffn_w_gate, ffn_w_up,
           ffn_w_down):
    batch, seq, d = x.shape
    assert d == D_MODEL and seq % SWA_WINDOW == 0
    m = batch * seq
    x2 = x.reshape(m, d)
    log_g = jnp.log1p(-jnp.exp2(-5.0 - jnp.arange(RET_HEADS, dtype=F32)))
    log_g = jnp.repeat(log_g, LANES)[None, :]

    layers = w_in.shape[0]

    def swa_order(t, axis):
        shape = t.shape
        t = t.reshape(shape[:axis] + (SWA_PAIRS, 2, SWA_GROUP, SWA_HEAD_DIM) + shape[axis + 1:])
        t = jnp.swapaxes(t, axis + 1, axis + 2)
        return t.reshape(shape)

    o_sk = _O_SWA + SWA_Q_HEADS * SWA_HEAD_DIM
    w_main = jnp.concatenate(
        [w_in[:, :, _O_GATE:_O_END], w_in[:, :, _O_RET:_O_GA], swa_order(w_in[:, :, _O_SWA:o_sk], 2),
         w_in[:, :, o_sk:_O_GATE]], axis=2).astype(BF16)
    w_ga = jnp.pad(w_in[:, :, _O_GA:_O_SWA], ((0, 0), (0, 0), (0, LANES - GLA_RANK))).astype(BF16)
    w2p = jnp.pad(gla_w_alpha2, ((0, 0), (0, LANES - GLA_RANK), (0, 0))).astype(BF16)
    w_br = jnp.concatenate([w_branch[:, :2], swa_order(w_branch[:, 2], 1)[:, None]], axis=1).astype(BF16)
    w_o = w_out.astype(BF16)
    w_fg, w_fu, w_fd = ffn_w_gate.astype(BF16), ffn_w_up.astype(BF16), ffn_w_down.astype(BF16)
    assert w_main.shape == (layers, d, P_TOTAL)

    h_next = None
    for l in range(DEPTH):
        if h_next is None:
            proj, ga = _inproj(x2, norm_mix_pre[l][None, :], w_main, w_ga, l, _tile(m, 1024), 1792)
        else:
            proj, ga = _inproj_normed(h_next, w_main, w_ga, l, _tile(m, 2048), 1792)
        o_ret = _retention(proj, log_g, ret_norm_w[l][None, :], ret_norm_b[l][None, :], batch, seq)
        o_gla = _gla(proj, ga, w2p[l], gla_b_alpha[l][None, :], gla_norm_w[l][None, :], batch, seq)
        o_swa = _swa(proj, attn_sinks[l], batch, seq)
        x2 = _merge(x2, proj, o_ret, o_gla, o_swa, w_br, w_o, norm_mix_post[l][None, :], l, _tile(m, 512))
        if l + 1 < DEPTH:
            x2, h_next = _ffn(x2, w_fg, w_fu, w_fd, norm_ffn_pre[l][None, :], norm_ffn_post[l][None, :], l,
                              _tile(m, 512), next_gain=norm_mix_pre[l + 1][None, :])
        else:
            x2 = _ffn(x2, w_fg, w_fu, w_fd, norm_ffn_pre[l][None, :], norm_ffn_post[l][None, :], l, _tile(m, 512))
    return x2.reshape(batch, seq, d)
```

```python
import functools

import jax
import jax.numpy as jnp
from jax import lax
from jax.experimental import pallas as pl
from jax.experimental.pallas import tpu as pltpu

F32 = jnp.float32
BF16 = jnp.bfloat16

D_MODEL = 1024
DEPTH = 2
RET_HEADS, RET_DK, RET_DV, RET_CHUNK = 4, 128, 256, 128
GLA_HEADS, GLA_DK, GLA_DV, GLA_RANK, GLA_CHUNK = 4, 128, 256, 16, 64
GLA_TAU, GLA_LOG_GATE_MIN = 16.0, -1.0
SWA_Q_HEADS, SWA_KV_HEADS, SWA_HEAD_DIM, SWA_WINDOW = 16, 4, 64, 128
N_BRANCH = 3
D_FF = 2816
NORM_EPS = 1e-6
GN_EPS = 1e-5

LANES = 128
VMEM_LIMIT_BYTES = 56 * 1024 * 1024

_O_RET, _O_GLA, _O_GA, _O_SWA, _O_GATE, _O_END = 0, 3072, 6144, 6160, 7696, 10768
P_GATE = 0
P_RQ, P_RK, P_RV, P_RG = 3072, 3584, 4096, 5120
P_GQ, P_GK, P_GV, P_GG = 6144, 6656, 7168, 8192
P_SQ, P_SK, P_SV = 9216, 10240, 10496
P_TOTAL = 10752


def _cparams(*sem):
    return pltpu.CompilerParams(dimension_semantics=sem, vmem_limit_bytes=VMEM_LIMIT_BYTES)


def _rms(x, w, eps):
    return x * lax.rsqrt(jnp.mean(x * x, axis=-1, keepdims=True) + eps) * w


def _dot(a, b):
    return jnp.dot(a, b, preferred_element_type=F32)


def _dot_nt(a, b):
    return lax.dot_general(a, b, (((1,), (1,)), ((), ())), preferred_element_type=F32)


def _silu_bf16(g):
    h = g * 0.5
    return h + h * jnp.tanh(h)


def _dot_tn(a, b):
    return lax.dot_general(a, b, (((0,), (0,)), ((), ())), preferred_element_type=F32)


def _pipeline3(n, first, middle, last, state, lanes):
    assert n % lanes == 0
    rounds = n // lanes

    def run_round(r, state, do_last, do_middle, do_first):
        for lane in range(lanes):
            if do_last:
                last(r - 2, lane)
            if do_middle:
                state = middle(r - 1, lane, state)
            if do_first:
                first(r, lane)
        return state

    for r in range(min(2, rounds + 2)):
        state = run_round(r, state, False, 0 <= r - 1 < rounds, r < rounds)
    if rounds > 2:
        state = lax.fori_loop(2, rounds, lambda r, s: run_round(r, s, True, True, True), state)
    for r in range(max(rounds, 2), rounds + 2):
        state = run_round(r, state, r - 2 < rounds, r - 1 < rounds, False)
    return state


def _pipeline3_ring(n, first, middle, last):
    def run_round(r, par):
        static = isinstance(r, int)
        if not static or r < n:
            first(r, par)
        if not static or 0 <= r - 1 < n:
            middle(r - 1, 1 - par)
        if not static or 0 <= r - 2 < n:
            last(r - 2, par)

    lo = 2
    hi = max(lo, n - (n - lo) % 2)
    for r in range(min(lo, n + 2)):
        run_round(r, r % 2)
    if hi > lo:
        def two_rounds(j, carry):
            r = lo + 2 * j
            run_round(r, 0)
            run_round(r + 1, 1)
            return carry

        lax.fori_loop(0, (hi - lo) // 2, two_rounds, 0)
    for r in range(hi, n + 2):
        run_round(r, r % 2)


def _inproj_kernel(x_ref, g_ref, w_ref, wga_ref, o_ref, oga_ref, h_ref):
    @pl.when(pl.program_id(1) == 0)
    def _():
        h = _rms(x_ref[...], g_ref[...], NORM_EPS).astype(BF16)
        h_ref[...] = h
        oga_ref[...] = _dot(h, wga_ref[...]).astype(BF16)

    o_ref[...] = _dot(h_ref[...], w_ref[...]).astype(BF16)


def _inproj(x2, gain, w_main, w_ga, layer, tm, tn):
    m, d = x2.shape
    n = w_main.shape[2]
    return pl.pallas_call(
        _inproj_kernel,
        grid=(m // tm, n // tn),
        in_specs=[
            pl.BlockSpec((tm, d), lambda i, j: (i, 0)),
            pl.BlockSpec((1, d), lambda i, j: (0, 0)),
            pl.BlockSpec((None, d, tn), lambda i, j: (layer, 0, j)),
            pl.BlockSpec((None, d, LANES), lambda i, j: (layer, 0, 0)),
        ],
        out_specs=[
            pl.BlockSpec((tm, tn), lambda i, j: (i, j)),
            pl.BlockSpec((tm, LANES), lambda i, j: (i, 0)),
        ],
        out_shape=[jax.ShapeDtypeStruct((m, n), BF16), jax.ShapeDtypeStruct((m, LANES), BF16)],
        scratch_shapes=[pltpu.VMEM((tm, d), BF16)],
        compiler_params=_cparams("arbitrary", "arbitrary"),
        name="inproj",
    )(x2, gain, w_main, w_ga)


def _inproj_normed_kernel(h_ref, w_ref, wga_ref, o_ref, oga_ref):
    @pl.when(pl.program_id(1) == 0)
    def _():
        oga_ref[...] = _dot(h_ref[...], wga_ref[...]).astype(BF16)

    o_ref[...] = _dot(h_ref[...], w_ref[...]).astype(BF16)


def _inproj_normed(h, w_main, w_ga, layer, tm, tn):
    m, d = h.shape
    n = w_main.shape[2]
    return pl.pallas_call(
        _inproj_normed_kernel,
        grid=(m // tm, n // tn),
        in_specs=[
            pl.BlockSpec((tm, d), lambda i, j: (i, 0)),
            pl.BlockSpec((None, d, tn), lambda i, j: (layer, 0, j)),
            pl.BlockSpec((None, d, LANES), lambda i, j: (layer, 0, 0)),
        ],
        out_specs=[
            pl.BlockSpec((tm, tn), lambda i, j: (i, j)),
            pl.BlockSpec((tm, LANES), lambda i, j: (i, 0)),
        ],
        out_shape=[jax.ShapeDtypeStruct((m, n), BF16), jax.ShapeDtypeStruct((m, LANES), BF16)],
        compiler_params=_cparams("arbitrary", "arbitrary"),
        name="inproj_normed",
    )(h, w_main, w_ga)


RET_LANES = 8


def _ret_kernel(lg_ref, p_ref, nw_ref, nb_ref, o_ref, *scratch, n_chunks, heads):
    c = RET_CHUNK
    row = lax.broadcasted_iota(jnp.int32, (c, c), 0)
    col = lax.broadcasted_iota(jnp.int32, (c, c), 1)
    rel = (row - col).astype(F32)
    rowf = row.astype(F32)
    scale = RET_DK ** -0.5
    lhs_refs, kv_refs, acc_refs = scratch[0::3], scratch[1::3], scratch[2::3]

    def rows(slot, lane):
        return pl.ds(pl.multiple_of((slot * RET_LANES + lane) * c, c), c)

    def one_head(h, carry):
        kcols = pl.ds(pl.multiple_of(h * RET_DK, RET_DK), RET_DK)
        q_cols = kcols
        k_cols = pl.ds(pl.multiple_of((heads + h) * RET_DK, RET_DK), RET_DK)
        v_cols = pl.ds(pl.multiple_of(2 * heads * RET_DK + h * RET_DV, RET_DV), RET_DV)
        g_cols = pl.ds(pl.multiple_of(2 * heads * RET_DK + (heads + h) * RET_DV, RET_DV), RET_DV)
        vcols = pl.ds(pl.multiple_of(h * RET_DV, RET_DV), RET_DV)
        lg = lg_ref[:, kcols]
        decay = jnp.where(row >= col, jnp.exp(rel * lg), 0.0) * scale
        xi = (jnp.exp((rowf + 1.0) * lg) * scale).astype(BF16)
        zeta = jnp.exp((c - 1.0 - rowf) * lg).astype(BF16)
        gamma_c = jnp.exp(c * lg)
        gamma_c = jnp.concatenate([gamma_c, gamma_c], axis=1)
        nw = nw_ref[:, vcols].astype(BF16)
        nb = nb_ref[:, vcols].astype(BF16)

        def scores(slot, lane):
            r = rows(slot, lane)
            qc = p_ref[r, q_cols]
            kc = p_ref[r, k_cols]
            s = _dot_nt(qc, kc) * decay
            lhs_refs[lane][slot] = jnp.concatenate([s.astype(BF16), qc * xi], axis=1)
            kv_refs[lane][slot] = _dot_tn(kc * zeta, p_ref[r, v_cols])

        def mix(slot, lane, state):
            rhs = jnp.concatenate([p_ref[rows(slot, lane), v_cols], state.astype(BF16)], axis=0)
            acc_refs[lane][slot] = _dot(lhs_refs[lane][slot], rhs)
            return gamma_c * state + kv_refs[lane][slot]

        def finish(slot, lane):
            r = rows(slot, lane)
            o = acc_refs[lane][slot]
            mu = jnp.mean(o, axis=-1, keepdims=True)
            d = o - mu
            var = jnp.mean(d * d, axis=-1, keepdims=True)
            y = (d * lax.rsqrt(var + GN_EPS)).astype(BF16)
            o_ref[r, vcols] = (y * nw + nb) * _silu_bf16(p_ref[r, g_cols])

        _pipeline3(n_chunks, scores, mix, finish, jnp.zeros((RET_DK, RET_DV), F32), RET_LANES)
        return carry

    lax.fori_loop(0, heads, one_head, 0)


def _retention(proj, log_g, norm_w, norm_b, batch, seq):
    m = proj.shape[0]
    h = RET_HEADS
    n_chunks = seq // RET_CHUNK
    kern = functools.partial(_ret_kernel, n_chunks=n_chunks, heads=h)
    return pl.pallas_call(
        kern,
        grid=(batch,),
        in_specs=[
            pl.BlockSpec((1, h * LANES), lambda b: (0, 0)),
            pl.BlockSpec((seq, P_GQ - P_RQ), lambda b: (b, P_RQ // (P_GQ - P_RQ))),
            pl.BlockSpec((1, h * RET_DV), lambda b: (0, 0)),
            pl.BlockSpec((1, h * RET_DV), lambda b: (0, 0)),
        ],
        out_specs=pl.BlockSpec((seq, h * RET_DV), lambda b: (b, 0)),
        out_shape=jax.ShapeDtypeStruct((m, h * RET_DV), BF16),
        scratch_shapes=[
            pltpu.VMEM((n_chunks // RET_LANES, RET_CHUNK, RET_CHUNK + RET_DK), BF16),
            pltpu.VMEM((n_chunks // RET_LANES, RET_DK, RET_DV), F32),
            pltpu.VMEM((n_chunks // RET_LANES, RET_CHUNK, RET_DV), F32),
        ] * RET_LANES,
        compiler_params=_cparams("arbitrary"),
        name="retention",
    )(log_g, proj, norm_w, norm_b)


GLA_LANES = 8


def _gla_kernel(ga_ref, w2_ref, ba_ref, p_ref, nw_ref, o_ref, la_ref, *scratch, n_pairs, heads):
    c = GLA_CHUNK
    c2 = 2 * c
    dk = GLA_DK
    z = _dot(ga_ref[...], w2_ref[...]) + ba_ref[...]
    log_sig = jnp.minimum(z, 0.0) - jnp.log(1.0 + jnp.exp(-jnp.abs(z)))
    la_ref[...] = jnp.maximum(log_sig * (1.0 / GLA_TAU), GLA_LOG_GATE_MIN)

    row = lax.broadcasted_iota(jnp.int32, (c2, c2), 0)
    col = lax.broadcasted_iota(jnp.int32, (c2, c2), 1)
    causal = (row >= col) & ((row >= c) == (col >= c))
    tri = jnp.where(causal, 1.0, 0.0).astype(BF16)
    scale = dk ** -0.5

    qd_ref, ki_ref, kd_ref, dl_ref = scratch[:4]
    a_refs, kv_refs, dc_refs, sb_refs = (scratch[4 + j::4] for j in range(4))

    def rows(slot, lane):
        return pl.ds(pl.multiple_of((slot * GLA_LANES + lane) * c2, c2), c2)

    def one_head(h, carry):
        kcols = pl.ds(pl.multiple_of(h * dk, dk), dk)
        q_cols = kcols
        k_cols = pl.ds(pl.multiple_of((heads + h) * dk, dk), dk)
        v_cols = pl.ds(pl.multiple_of(2 * heads * dk + h * GLA_DV, GLA_DV), GLA_DV)
        g_cols = pl.ds(pl.multiple_of(2 * heads * dk + (heads + h) * GLA_DV, GLA_DV), GLA_DV)
        vcols = pl.ds(pl.multiple_of(h * GLA_DV, GLA_DV), GLA_DV)
        nw = nw_ref[:, vcols].astype(BF16)

        def decays(p, carry):
            r = pl.ds(pl.multiple_of(p * c2, c2), c2)
            la = la_ref[r, kcols]
            hi = la.astype(BF16)
            rest = la - hi.astype(F32)
            mid = rest.astype(BF16)
            low = (rest - mid.astype(F32)).astype(BF16)
            parts = _dot(tri, jnp.concatenate([hi, mid, low], axis=1))
            bcum = parts[:, :dk] + parts[:, dk:2 * dk] + parts[:, 2 * dk:]
            eb = jnp.exp(bcum)
            k_inv = p_ref[r, k_cols].astype(F32) * (1.0 / eb)
            qd_ref[r, :] = (p_ref[r, q_cols].astype(F32) * (eb * scale)).astype(BF16)
            ki_ref[r, :] = k_inv.astype(BF16)
            k_dec = []
            for half in range(2):
                lo, up = half * c, (half + 1) * c
                d_last = eb[up - 1:up, :]
                dl_ref[p, half] = jnp.broadcast_to(d_last, (8, dk))
                k_dec.append((k_inv[lo:up] * d_last).astype(BF16))
            kd_ref[r, :] = jnp.concatenate(k_dec, axis=0)
            return carry

        lax.fori_loop(0, n_pairs, decays, 0, unroll=8)

        def prepare(slot, lane):
            r = rows(slot, lane)
            a = _dot_nt(qd_ref[r, :], ki_ref[r, :])
            a_refs[lane][slot] = jnp.where(causal, a, 0.0).astype(BF16)
            k_dec = kd_ref[r, :]
            vc = p_ref[r, v_cols]
            for half in range(2):
                lo, up = half * c, (half + 1) * c
                kv_refs[lane][slot, half] = _dot_tn(k_dec[lo:up], vc[lo:up])
                d_last = dl_ref[slot * GLA_LANES + lane, half][0:1, :]
                dc_refs[lane][slot, half] = jnp.broadcast_to(d_last, (dk, dk)).T

        def scan(slot, lane, state):
            for half in range(2):
                sb_refs[lane][slot, half] = state.astype(BF16)
                dcol = dc_refs[lane][slot, half]
                state = jnp.concatenate([dcol, dcol], axis=1) * state + kv_refs[lane][slot, half]
            return state

        def finish(slot, lane):
            r = rows(slot, lane)
            q_dec = qd_ref[r, :]
            intra = _dot(a_refs[lane][slot], p_ref[r, v_cols])
            o = jnp.concatenate(
                [intra[i * c:(i + 1) * c] + _dot(q_dec[i * c:(i + 1) * c], sb_refs[lane][slot, i])
                 for i in range(2)], axis=0)
            y = (o * lax.rsqrt(jnp.mean(o * o, axis=-1, keepdims=True) + NORM_EPS)).astype(BF16)
            o_ref[r, vcols] = (y * nw) * _silu_bf16(p_ref[r, g_cols])

        _pipeline3(n_pairs, prepare, scan, finish, jnp.zeros((dk, GLA_DV), F32), GLA_LANES)
        return carry

    lax.fori_loop(0, heads, one_head, 0)


def _gla(proj, ga, w2p, b_alpha, norm_w, batch, seq):
    m = proj.shape[0]
    h = GLA_HEADS
    n_pairs = seq // (2 * GLA_CHUNK)
    slots = n_pairs // GLA_LANES
    kern = functools.partial(_gla_kernel, n_pairs=n_pairs, heads=h)
    return pl.pallas_call(
        kern,
        grid=(batch,),
        in_specs=[
            pl.BlockSpec((seq, LANES), lambda b: (b, 0)),
            pl.BlockSpec((LANES, h * GLA_DK), lambda b: (0, 0)),
            pl.BlockSpec((1, h * GLA_DK), lambda b: (0, 0)),
            pl.BlockSpec((seq, P_SQ - P_GQ), lambda b: (b, P_GQ // (P_SQ - P_GQ))),
            pl.BlockSpec((1, h * GLA_DV), lambda b: (0, 0)),
        ],
        out_specs=pl.BlockSpec((seq, h * GLA_DV), lambda b: (b, 0)),
        out_shape=jax.ShapeDtypeStruct((m, h * GLA_DV), BF16),
        scratch_shapes=[
            pltpu.VMEM((seq, h * GLA_DK), F32),
            pltpu.VMEM((seq, GLA_DK), BF16),
            pltpu.VMEM((seq, GLA_DK), BF16),
            pltpu.VMEM((seq, GLA_DK), BF16),
            pltpu.VMEM((n_pairs, 2, 8, GLA_DK), F32),
        ] + [
            pltpu.VMEM((slots, 2 * GLA_CHUNK, 2 * GLA_CHUNK), BF16),
            pltpu.VMEM((slots, 2, GLA_DK, GLA_DV), F32),
            pltpu.VMEM((slots, 2, GLA_DK, GLA_DK), F32),
            pltpu.VMEM((slots, 2, GLA_DK, GLA_DV), BF16),
        ] * GLA_LANES,
        compiler_params=_cparams("arbitrary"),
        name="gla",
    )(ga, w2p, b_alpha, proj, norm_w)


def _swa_slope(hq):
    return 2.0 ** (-8.0 * (hq + 1) / SWA_Q_HEADS)


SWA_GROUP = SWA_Q_HEADS // SWA_KV_HEADS
SWA_PAIRS = SWA_KV_HEADS // 2


def _swa_head(pair, half, g):
    return (2 * pair + half) * SWA_GROUP + g


def _swa_head_order():
    return [_swa_head(pair, half, g) for pair in range(SWA_PAIRS) for g in range(SWA_GROUP) for half in range(2)]


def _swa_kernel(sink_ref, p_ref, o_ref, bias_ref, lg_ref, acc_ref, es_ref, *, n_blocks):
    w = SWA_WINDOW
    dh = SWA_HEAD_DIM
    gw = SWA_GROUP * w
    nkv = SWA_KV_HEADS

    @pl.when(pl.program_id(0) == 0)
    def _():
        kpos = lax.broadcasted_iota(jnp.int32, (2 * w, w), 0)
        qpos = lax.broadcasted_iota(jnp.int32, (2 * w, w), 1) + w
        dist = qpos - kpos
        band = (dist >= 0) & (dist < w)
        distf = dist.astype(F32)
        for pair in range(SWA_PAIRS):
            for half in range(2):
                for g in range(SWA_GROUP):
                    alibi = -_swa_slope(_swa_head(pair, half, g)) * distf
                    kv = 2 * pair + half
                    bias_ref[kv, :, g * w:(g + 1) * w] = jnp.where(band, alibi, -jnp.inf)
                    bias_ref[nkv + kv, :, g * w:(g + 1) * w] = jnp.where(band & (kpos >= w), alibi, -jnp.inf)

    lane = lax.broadcasted_iota(jnp.int32, (1, 2 * dh), 1)
    in_half = [lane < dh, lane >= dh]
    sub = lax.broadcasted_iota(jnp.int32, (2 * dh, 1), 0)
    sinks = [jnp.concatenate([jnp.full((1, w), sink_ref[_swa_head(kv // 2, kv % 2, g)], F32)
                              for g in range(SWA_GROUP)], axis=1) for kv in range(nkv)]

    def block_rows(n):
        return pl.ds(n * w if isinstance(n, int) else pl.multiple_of(n * w, w), w)

    q_rows = block_rows

    k_off = SWA_Q_HEADS * dh
    v_off = k_off + nkv * dh

    def kv_block(off, n, pair):
        prev = max(n - 1, 0) if isinstance(n, int) else n - 1
        cols = slice(off + pair * 2 * dh, off + (pair + 1) * 2 * dh)
        return jnp.concatenate([p_ref[block_rows(prev), cols], p_ref[block_rows(n), cols]], axis=0)

    def scores(n, par):
        table = nkv if isinstance(n, int) and n == 0 else 0
        for pair in range(SWA_PAIRS):
            qst = jnp.concatenate(
                [p_ref[q_rows(n), (pair * SWA_GROUP + g) * 2 * dh:(pair * SWA_GROUP + g + 1) * 2 * dh]
                 for g in range(SWA_GROUP)], axis=0)
            kp = kv_block(k_off, n, pair) * (dh ** -0.5)
            for half in range(2):
                kv = 2 * pair + half
                kh = jnp.where(in_half[half], kp, jnp.zeros_like(kp))
                lg_ref[par, kv] = _dot_nt(kh, qst) + bias_ref[table + kv]

    def probs(n, par):
        for pair in range(SWA_PAIRS):
            vp = kv_block(v_off, n, pair)
            for half in range(2):
                kv = 2 * pair + half
                logits = lg_ref[par, kv]
                mx = jnp.maximum(jnp.max(logits, axis=0, keepdims=True), sinks[kv])
                p = jnp.exp(logits - mx).astype(BF16)
                vh = jnp.where(in_half[half], vp, jnp.ones_like(vp))
                acc_ref[par, kv] = _dot_tn(vh, p)
                es_ref[par, kv] = jnp.broadcast_to(jnp.exp(sinks[kv] - mx), (8, gw))

    def outputs(n, par):
        for pair in range(SWA_PAIRS):
            out_t = None
            for half in range(2):
                kv = 2 * pair + half
                acc = acc_ref[par, kv]
                other = (1 - half) * dh
                den = acc[other:other + 1, :] + es_ref[par, kv][0:1, :]
                acc = acc * (1.0 / den)
                out_t = acc if out_t is None else jnp.where(sub < dh, out_t, acc)
            for g in range(SWA_GROUP):
                blk = pair * SWA_GROUP + g
                o_ref[q_rows(n), blk * 2 * dh:(blk + 1) * 2 * dh] = out_t[:, g * w:(g + 1) * w].T.astype(BF16)

    _pipeline3_ring(n_blocks, scores, probs, outputs)


def _swa(proj, sinks, batch, seq):
    m = proj.shape[0]
    w = SWA_WINDOW
    qw = SWA_Q_HEADS * SWA_HEAD_DIM
    kern = functools.partial(_swa_kernel, n_blocks=seq // w)
    return pl.pallas_call(
        kern,
        grid=(batch,),
        in_specs=[
            pl.BlockSpec(memory_space=pltpu.SMEM),
            pl.BlockSpec((seq, P_TOTAL - P_SQ), lambda b: (b, P_SQ // (P_TOTAL - P_SQ))),
        ],
        out_specs=pl.BlockSpec((seq, qw), lambda b: (b, 0)),
        out_shape=jax.ShapeDtypeStruct((m, qw), BF16),
        scratch_shapes=[
            pltpu.VMEM((2 * SWA_KV_HEADS, 2 * w, SWA_GROUP * w), F32),
            pltpu.VMEM((2, SWA_KV_HEADS, 2 * w, SWA_GROUP * w), F32),
            pltpu.VMEM((2, SWA_KV_HEADS, 2 * SWA_HEAD_DIM, SWA_GROUP * w), F32),
            pltpu.VMEM((2, SWA_KV_HEADS, 8, SWA_GROUP * w), F32),
        ],
        compiler_params=_cparams("arbitrary"),
        name="swa",
    )(sinks, proj)


def _merge_kernel(x_ref, gl_ref, r_ref, g_ref, s_ref, wb_ref, wo_ref, nw_ref, o_ref):
    d = D_MODEL
    merged = None
    for n, br in enumerate((r_ref, g_ref, s_ref)):
        p = _dot(br[...], wb_ref[n])
        gate = jax.nn.sigmoid(gl_ref[:, n * d:(n + 1) * d].astype(F32))
        merged = gate * p if merged is None else merged + gate * p
    y = _dot(merged.astype(BF16), wo_ref[...])
    o_ref[...] = x_ref[...] + _rms(y, nw_ref[...], NORM_EPS)


def _merge(x2, proj, o_ret, o_gla, o_swa, wb, wo, nw, layer, tm):
    m, d = x2.shape
    const = pl.Buffered(1)
    return pl.pallas_call(
        _merge_kernel,
        grid=(m // tm,),
        in_specs=[
            pl.BlockSpec((tm, d), lambda i: (i, 0)),
            pl.BlockSpec((tm, N_BRANCH * d), lambda i: (i, P_GATE // (N_BRANCH * d))),
            pl.BlockSpec((tm, d), lambda i: (i, 0)),
            pl.BlockSpec((tm, d), lambda i: (i, 0)),
            pl.BlockSpec((tm, d), lambda i: (i, 0)),
            pl.BlockSpec((None, N_BRANCH, d, d), lambda i: (layer, 0, 0, 0), pipeline_mode=const),
            pl.BlockSpec((None, d, d), lambda i: (layer, 0, 0), pipeline_mode=const),
            pl.BlockSpec((1, d), lambda i: (0, 0)),
        ],
        out_specs=pl.BlockSpec((tm, d), lambda i: (i, 0)),
        out_shape=jax.ShapeDtypeStruct((m, d), F32),
        compiler_params=_cparams("arbitrary"),
        name="merge",
    )(x2, proj, o_ret, o_gla, o_swa, wb, wo, nw)


def _ffn_kernel(x_ref, wg_ref, wu_ref, wd_ref, npre_ref, npost_ref, *rest):
    x = x_ref[...]
    h = _rms(x, npre_ref[...], NORM_EPS).astype(BF16)
    gate = _dot(h, wg_ref[...])
    up = _dot(h, wu_ref[...])
    act = (gate * jax.nn.sigmoid(gate) * up).astype(BF16)
    f = _dot(act, wd_ref[...])
    out = x + _rms(f, npost_ref[...], NORM_EPS)
    if len(rest) == 1:
        (o_ref,) = rest
    else:
        nnext_ref, o_ref, hn_ref = rest
        hn_ref[...] = _rms(out, nnext_ref[...], NORM_EPS).astype(BF16)
    o_ref[...] = out


def _ffn(x2, wg, wu, wd, npre, npost, layer, tm, next_gain=None):
    m, d = x2.shape
    ff = wg.shape[2]
    const = pl.Buffered(1)
    row_spec = pl.BlockSpec((tm, d), lambda i: (i, 0))
    vec_spec = pl.BlockSpec((1, d), lambda i: (0, 0))
    in_specs = [
        row_spec,
        pl.BlockSpec((None, d, ff), lambda i: (layer, 0, 0), pipeline_mode=const),
        pl.BlockSpec((None, d, ff), lambda i: (layer, 0, 0), pipeline_mode=const),
        pl.BlockSpec((None, ff, d), lambda i: (layer, 0, 0), pipeline_mode=const),
        vec_spec,
        vec_spec,
    ]
    args = [x2, wg, wu, wd, npre, npost]
    out_specs, out_shape = row_spec, jax.ShapeDtypeStruct((m, d), F32)
    if next_gain is not None:
        in_specs.append(vec_spec)
        args.append(next_gain)
        out_specs = [row_spec, row_spec]
        out_shape = [out_shape, jax.ShapeDtypeStruct((m, d), BF16)]
    return pl.pallas_call(
        _ffn_kernel,
        grid=(m // tm,),
        in_specs=in_specs,
        out_specs=out_specs,
        out_shape=out_shape,
        compiler_params=_cparams("arbitrary"),
        name="ffn",
    )(*args)


def _tile(m, want):
    t = min(want, m)
    assert m % t == 0
    return t


def kernel(x, norm_mix_pre, norm_mix_post, w_in, ret_norm_w, ret_norm_b, gla_w_alpha2, gla_b_alpha,
           gla_norm_w, attn_sinks, w_branch, w_out, norm_ffn_pre, norm_ffn_post, ffn_w_gate, ffn_w_up,
           ffn_w_down):
    batch, seq, d = x.shape
    assert d == D_MODEL and seq % SWA_WINDOW == 0
    m = batch * seq
    x2 = x.reshape(m, d)
    log_g = jnp.log1p(-jnp.exp2(-5.0 - jnp.arange(RET_HEADS, dtype=F32)))
    log_g = jnp.repeat(log_g, LANES)[None, :]

    layers = w_in.shape[0]

    def swa_order(t, axis):
        shape = t.shape
        t = t.reshape(shape[:axis] + (SWA_PAIRS, 2, SWA_GROUP, SWA_HEAD_DIM) + shape[axis + 1:])
        t = jnp.swapaxes(t, axis + 1, axis + 2)
        return t.reshape(shape)

    o_sk = _O_SWA + SWA_Q_HEADS * SWA_HEAD_DIM
    w_main = jnp.concatenate(
        [w_in[:, :, _O_GATE:_O_END], w_in[:, :, _O_RET:_O_GA], swa_order(w_in[:, :, _O_SWA:o_sk], 2),
         w_in[:, :, o_sk:_O_GATE]], axis=2).astype(BF16)
    w_ga = jnp.pad(w_in[:, :, _O_GA:_O_SWA], ((0, 0), (0, 0), (0, LANES - GLA_RANK))).astype(BF16)
    w2p = jnp.pad(gla_w_alpha2, ((0, 0), (0, LANES - GLA_RANK), (0, 0))).astype(BF16)
    w_br = jnp.concatenate([w_branch[:, :2], swa_order(w_branch[:, 2], 1)[:, None]], axis=1).astype(BF16)
    w_o = w_out.astype(BF16)
    w_fg, w_fu, w_fd = ffn_w_gate.astype(BF16), ffn_w_up.astype(BF16), ffn_w_down.astype(BF16)
    assert w_main.shape == (layers, d, P_TOTAL)

    h_next = None
    for l in range(DEPTH):
        if h_next is None:
            proj, ga = _inproj(x2, norm_mix_pre[l][None, :], w_main, w_ga, l, _tile(m, 1024), 1792)
        else:
            proj, ga = _inproj_normed(h_next, w_main, w_ga, l, _tile(m, 2048), 1792)
        o_ret = _retention(proj, log_g, ret_norm_w[l][None, :], ret_norm_b[l][None, :], batch, seq)
        o_gla = _gla(proj, ga, w2p[l], gla_b_alpha[l][None, :], gla_norm_w[l][None, :], batch, seq)
        o_swa = _swa(proj, attn_sinks[l], batch, seq)
        x2 = _merge(x2, proj, o_ret, o_gla, o_swa, w_br, w_o, norm_mix_post[l][None, :], l, _tile(m, 512))
        if l + 1 < DEPTH:
            x2, h_next = _ffn(x2, w_fg, w_fu, w_fd, norm_ffn_pre[l][None, :], norm_ffn_post[l][None, :], l,
                              _tile(m, 512), next_gain=norm_mix_pre[l + 1][None, :])
        else:
            x2 = _ffn(x2, w_fg, w_fu, w_fd, norm_ffn_pre[l][None, :], norm_ffn_post[l][None, :], l, _tile(m, 512))
    return x2.reshape(batch, seq, d)
```

```python
import functools

import jax
import jax.numpy as jnp
from jax import lax
from jax.experimental import pallas as pl
from jax.experimental.pallas import tpu as pltpu

F32 = jnp.float32
BF16 = jnp.bfloat16

D_MODEL = 1024
DEPTH = 2
RET_HEADS, RET_DK, RET_DV, RET_CHUNK = 4, 128, 256, 128
GLA_HEADS, GLA_DK, GLA_DV, GLA_RANK, GLA_CHUNK = 4, 128, 256, 16, 64
GLA_TAU, GLA_LOG_GATE_MIN = 16.0, -1.0
SWA_Q_HEADS, SWA_KV_HEADS, SWA_HEAD_DIM, SWA_WINDOW = 16, 4, 64, 128
N_BRANCH = 3
D_FF = 2816
NORM_EPS = 1e-6
GN_EPS = 1e-5

LANES = 128
VMEM_LIMIT_BYTES = 56 * 1024 * 1024

_O_RET, _O_GLA, _O_GA, _O_SWA, _O_GATE, _O_END = 0, 3072, 6144, 6160, 7696, 10768
P_GATE = 0
P_RQ, P_RK, P_RV, P_RG = 3072, 3584, 4096, 5120
P_GQ, P_GK, P_GV, P_GG = 6144, 6656, 7168, 8192
P_SQ, P_SK, P_SV = 9216, 10240, 10496
P_TOTAL = 10752


def _cparams(*sem):
    return pltpu.CompilerParams(dimension_semantics=sem, vmem_limit_bytes=VMEM_LIMIT_BYTES)


def _rms(x, w, eps):
    return x * lax.rsqrt(jnp.mean(x * x, axis=-1, keepdims=True) + eps) * w


def _dot(a, b):
    return jnp.dot(a, b, preferred_element_type=F32)


def _dot_nt(a, b):
    return lax.dot_general(a, b, (((1,), (1,)), ((), ())), preferred_element_type=F32)


def _silu_bf16(g):
    h = g * 0.5
    return h + h * jnp.tanh(h)


def _dot_tn(a, b):
    return lax.dot_general(a, b, (((0,), (0,)), ((), ())), preferred_element_type=F32)


def _pipeline3(n, first, middle, last, state, lanes):
    assert n % lanes == 0
    rounds = n // lanes

    def run_round(r, state, do_last, do_middle, do_first):
        for lane in range(lanes):
            if do_last:
                last(r - 2, lane)
            if do_middle:
                state = middle(r - 1, lane, state)
            if do_first:
                first(r, lane)
        return state

    for r in range(min(2, rounds + 2)):
        state = run_round(r, state, False, 0 <= r - 1 < rounds, r < rounds)
    if rounds > 2:
        state = lax.fori_loop(2, rounds, lambda r, s: run_round(r, s, True, True, True), state)
    for r in range(max(rounds, 2), rounds + 2):
        state = run_round(r, state, r - 2 < rounds, r - 1 < rounds, False)
    return state


def _pipeline3_ring(n, first, middle, last):
    def run_round(r, par):
        static = isinstance(r, int)
        if not static or r < n:
            first(r, par)
        if not static or 0 <= r - 1 < n:
            middle(r - 1, 1 - par)
        if not static or 0 <= r - 2 < n:
            last(r - 2, par)

    lo = 2
    hi = max(lo, n - (n - lo) % 2)
    for r in range(min(lo, n + 2)):
        run_round(r, r % 2)
    if hi > lo:
        def two_rounds(j, carry):
            r = lo + 2 * j
            run_round(r, 0)
            run_round(r + 1, 1)
            return carry

        lax.fori_loop(0, (hi - lo) // 2, two_rounds, 0)
    for r in range(hi, n + 2):
        run_round(r, r % 2)


def _inproj_kernel(x_ref, g_ref, w_ref, wga_ref, o_ref, oga_ref, h_ref):
    @pl.when(pl.program_id(1) == 0)
    def _():
        h = _rms(x_ref[...], g_ref[...], NORM_EPS).astype(BF16)
        h_ref[...] = h
        oga_ref[...] = _dot(h, wga_ref[...]).astype(BF16)

    o_ref[...] = _dot(h_ref[...], w_ref[...]).astype(BF16)


def _inproj(x2, gain, w_main, w_ga, layer, tm, tn):
    m, d = x2.shape
    n = w_main.shape[2]
    return pl.pallas_call(
        _inproj_kernel,
        grid=(m // tm, n // tn),
        in_specs=[
            pl.BlockSpec((tm, d), lambda i, j: (i, 0)),
            pl.BlockSpec((1, d), lambda i, j: (0, 0)),
            pl.BlockSpec((None, d, tn), lambda i, j: (layer, 0, j)),
            pl.BlockSpec((None, d, LANES), lambda i, j: (layer, 0, 0)),
        ],
        out_specs=[
            pl.BlockSpec((tm, tn), lambda i, j: (i, j)),
            pl.BlockSpec((tm, LANES), lambda i, j: (i, 0)),
        ],
        out_shape=[jax.ShapeDtypeStruct((m, n), BF16), jax.ShapeDtypeStruct((m, LANES), BF16)],
        scratch_shapes=[pltpu.VMEM((tm, d), BF16)],
        compiler_params=_cparams("arbitrary", "arbitrary"),
        name="inproj",
    )(x2, gain, w_main, w_ga)


def _inproj_normed_kernel(h_ref, w_ref, wga_ref, o_ref, oga_ref):
    @pl.when(pl.program_id(1) == 0)
    def _():
        oga_ref[...] = _dot(h_ref[...], wga_ref[...]).astype(BF16)

    o_ref[...] = _dot(h_ref[...], w_ref[...]).astype(BF16)


def _inproj_normed(h, w_main, w_ga, layer, tm, tn):
    m, d = h.shape
    n = w_main.shape[2]
    return pl.pallas_call(
        _inproj_normed_kernel,
        grid=(m // tm, n // tn),
        in_specs=[
            pl.BlockSpec((tm, d), lambda i, j: (i, 0)),
            pl.BlockSpec((None, d, tn), lambda i, j: (layer, 0, j)),
            pl.BlockSpec((None, d, LANES), lambda i, j: (layer, 0, 0)),
        ],
        out_specs=[
            pl.BlockSpec((tm, tn), lambda i, j: (i, j)),
            pl.BlockSpec((tm, LANES), lambda i, j: (i, 0)),
        ],
        out_shape=[jax.ShapeDtypeStruct((m, n), BF16), jax.ShapeDtypeStruct((m, LANES), BF16)],
        compiler_params=_cparams("arbitrary", "arbitrary"),
        name="inproj_normed",
    )(h, w_main, w_ga)


RET_LANES = 8


def _ret_kernel(lg_ref, p_ref, nw_ref, nb_ref, o_ref, *scratch, n_chunks, heads):
    c = RET_CHUNK
    row = lax.broadcasted_iota(jnp.int32, (c, c), 0)
    col = lax.broadcasted_iota(jnp.int32, (c, c), 1)
    rel = (row - col).astype(F32)
    rowf = row.astype(F32)
    scale = RET_DK ** -0.5
    lhs_refs, kv_refs, acc_refs = scratch[0::3], scratch[1::3], scratch[2::3]

    def rows(slot, lane):
        return pl.ds(pl.multiple_of((slot * RET_LANES + lane) * c, c), c)

    def one_head(h, carry):
        kcols = pl.ds(pl.multiple_of(h * RET_DK, RET_DK), RET_DK)
        q_cols = kcols
        k_cols = pl.ds(pl.multiple_of((heads + h) * RET_DK, RET_DK), RET_DK)
        v_cols = pl.ds(pl.multiple_of(2 * heads * RET_DK + h * RET_DV, RET_DV), RET_DV)
        g_cols = pl.ds(pl.multiple_of(2 * heads * RET_DK + (heads + h) * RET_DV, RET_DV), RET_DV)
        vcols = pl.ds(pl.multiple_of(h * RET_DV, RET_DV), RET_DV)
        lg = lg_ref[:, kcols]
        decay = jnp.where(row >= col, jnp.exp(rel * lg), 0.0) * scale
        xi = (jnp.exp((rowf + 1.0) * lg) * scale).astype(BF16)
        zeta = jnp.exp((c - 1.0 - rowf) * lg).astype(BF16)
        gamma_c = jnp.exp(c * lg)
        gamma_c = jnp.concatenate([gamma_c, gamma_c], axis=1)
        nw = nw_ref[:, vcols].astype(BF16)
        nb = nb_ref[:, vcols].astype(BF16)

        def scores(slot, lane):
            r = rows(slot, lane)
            qc = p_ref[r, q_cols]
            kc = p_ref[r, k_cols]
            s = _dot_nt(qc, kc) * decay
            lhs_refs[lane][slot] = jnp.concatenate([s.astype(BF16), qc * xi], axis=1)
            kv_refs[lane][slot] = _dot_tn(kc * zeta, p_ref[r, v_cols])

        def mix(slot, lane, state):
            rhs = jnp.concatenate([p_ref[rows(slot, lane), v_cols], state.astype(BF16)], axis=0)
            acc_refs[lane][slot] = _dot(lhs_refs[lane][slot], rhs)
            return gamma_c * state + kv_refs[lane][slot]

        def finish(slot, lane):
            r = rows(slot, lane)
            o = acc_refs[lane][slot]
            mu = jnp.mean(o, axis=-1, keepdims=True)
            d = o - mu
            var = jnp.mean(d * d, axis=-1, keepdims=True)
            y = (d * lax.rsqrt(var + GN_EPS)).astype(BF16)
            o_ref[r, vcols] = (y * nw + nb) * _silu_bf16(p_ref[r, g_cols])

        _pipeline3(n_chunks, scores, mix, finish, jnp.zeros((RET_DK, RET_DV), F32), RET_LANES)
        return carry

    lax.fori_loop(0, heads, one_head, 0)


def _retention(proj, log_g, norm_w, norm_b, batch, seq):
    m = proj.shape[0]
    h = RET_HEADS
    n_chunks = seq // RET_CHUNK
    kern = functools.partial(_ret_kernel, n_chunks=n_chunks, heads=h)
    return pl.pallas_call(
        kern,
        grid=(batch,),
        in_specs=[
            pl.BlockSpec((1, h * LANES), lambda b: (0, 0)),
            pl.BlockSpec((seq, P_GQ - P_RQ), lambda b: (b, P_RQ // (P_GQ - P_RQ))),
            pl.BlockSpec((1, h * RET_DV), lambda b: (0, 0)),
            pl.BlockSpec((1, h * RET_DV), lambda b: (0, 0)),
        ],
        out_specs=pl.BlockSpec((seq, h * RET_DV), lambda b: (b, 0)),
        out_shape=jax.ShapeDtypeStruct((m, h * RET_DV), BF16),
        scratch_shapes=[
            pltpu.VMEM((n_chunks // RET_LANES, RET_CHUNK, RET_CHUNK + RET_DK), BF16),
            pltpu.VMEM((n_chunks // RET_LANES, RET_DK, RET_DV), F32),
            pltpu.VMEM((n_chunks // RET_LANES, RET_CHUNK, RET_DV), F32),
        ] * RET_LANES,
        compiler_params=_cparams("arbitrary"),
        name="retention",
    )(log_g, proj, norm_w, norm_b)


GLA_LANES = 8


def _gla_kernel(ga_ref, w2_ref, ba_ref, p_ref, nw_ref, o_ref, la_ref, *scratch, n_pairs, heads):
    c = GLA_CHUNK
    c2 = 2 * c
    dk = GLA_DK
    z = _dot(ga_ref[...], w2_ref[...]) + ba_ref[...]
    la_ref[...] = jnp.maximum(jnp.log(1.0 + jnp.exp(-z)) * (-1.0 / GLA_TAU), GLA_LOG_GATE_MIN)

    row = lax.broadcasted_iota(jnp.int32, (c2, c2), 0)
    col = lax.broadcasted_iota(jnp.int32, (c2, c2), 1)
    causal = (row >= col) & ((row >= c) == (col >= c))
    tri = jnp.where(causal, 1.0, 0.0).astype(BF16)
    scale = dk ** -0.5

    qd_ref, ki_ref, kd_ref, dl_ref = scratch[:4]
    a_refs, kv_refs, dc_refs, sb_refs = (scratch[4 + j::4] for j in range(4))

    def rows(slot, lane):
        return pl.ds(pl.multiple_of((slot * GLA_LANES + lane) * c2, c2), c2)

    def one_head(h, carry):
        kcols = pl.ds(pl.multiple_of(h * dk, dk), dk)
        q_cols = kcols
        k_cols = pl.ds(pl.multiple_of((heads + h) * dk, dk), dk)
        v_cols = pl.ds(pl.multiple_of(2 * heads * dk + h * GLA_DV, GLA_DV), GLA_DV)
        g_cols = pl.ds(pl.multiple_of(2 * heads * dk + (heads + h) * GLA_DV, GLA_DV), GLA_DV)
        vcols = pl.ds(pl.multiple_of(h * GLA_DV, GLA_DV), GLA_DV)
        nw = nw_ref[:, vcols].astype(BF16)

        def decays(p, carry):
            r = pl.ds(pl.multiple_of(p * c2, c2), c2)
            la = la_ref[r, kcols]
            hi = la.astype(BF16)
            rest = la - hi.astype(F32)
            mid = rest.astype(BF16)
            low = (rest - mid.astype(F32)).astype(BF16)
            parts = _dot(tri, jnp.concatenate([hi, mid, low], axis=1))
            bcum = parts[:, :dk] + parts[:, dk:2 * dk] + parts[:, 2 * dk:]
            eb = jnp.exp(bcum)
            k_inv = p_ref[r, k_cols].astype(F32) * (1.0 / eb)
            qd_ref[r, :] = (p_ref[r, q_cols].astype(F32) * (eb * scale)).astype(BF16)
            ki_ref[r, :] = k_inv.astype(BF16)
            k_dec = []
            for half in range(2):
                lo, up = half * c, (half + 1) * c
                d_last = eb[up - 1:up, :]
                dl_ref[p, half] = jnp.broadcast_to(d_last, (8, dk))
                k_dec.append((k_inv[lo:up] * d_last).astype(BF16))
            kd_ref[r, :] = jnp.concatenate(k_dec, axis=0)
            return carry

        lax.fori_loop(0, n_pairs, decays, 0, unroll=8)

        def prepare(slot, lane):
            r = rows(slot, lane)
            a = _dot_nt(qd_ref[r, :], ki_ref[r, :])
            a_refs[lane][slot] = jnp.where(causal, a, 0.0).astype(BF16)
            k_dec = kd_ref[r, :]
            vc = p_ref[r, v_cols]
            for half in range(2):
                lo, up = half * c, (half + 1) * c
                kv_refs[lane][slot, half] = _dot_tn(k_dec[lo:up], vc[lo:up])
                d_last = dl_ref[slot * GLA_LANES + lane, half][0:1, :]
                dc_refs[lane][slot, half] = jnp.broadcast_to(d_last, (dk, dk)).T

        def scan(slot, lane, state):
            for half in range(2):
                sb_refs[lane][slot, half] = state.astype(BF16)
                dcol = dc_refs[lane][slot, half]
                state = jnp.concatenate([dcol, dcol], axis=1) * state + kv_refs[lane][slot, half]
            return state

        def finish(slot, lane):
            r = rows(slot, lane)
            q_dec = qd_ref[r, :]
            intra = _dot(a_refs[lane][slot], p_ref[r, v_cols])
            o = jnp.concatenate(
                [intra[i * c:(i + 1) * c] + _dot(q_dec[i * c:(i + 1) * c], sb_refs[lane][slot, i])
                 for i in range(2)], axis=0)
            y = (o * lax.rsqrt(jnp.mean(o * o, axis=-1, keepdims=True) + NORM_EPS)).astype(BF16)
            o_ref[r, vcols] = (y * nw) * _silu_bf16(p_ref[r, g_cols])

        _pipeline3(n_pairs, prepare, scan, finish, jnp.zeros((dk, GLA_DV), F32), GLA_LANES)
        return carry

    lax.fori_loop(0, heads, one_head, 0)


def _gla(proj, ga, w2p, b_alpha, norm_w, batch, seq):
    m = proj.shape[0]
    h = GLA_HEADS
    n_pairs = seq // (2 * GLA_CHUNK)
    slots = n_pairs // GLA_LANES
    kern = functools.partial(_gla_kernel, n_pairs=n_pairs, heads=h)
    return pl.pallas_call(
        kern,
        grid=(batch,),
        in_specs=[
            pl.BlockSpec((seq, LANES), lambda b: (b, 0)),
            pl.BlockSpec((LANES, h * GLA_DK), lambda b: (0, 0)),
            pl.BlockSpec((1, h * GLA_DK), lambda b: (0, 0)),
            pl.BlockSpec((seq, P_SQ - P_GQ), lambda b: (b, P_GQ // (P_SQ - P_GQ))),
            pl.BlockSpec((1, h * GLA_DV), lambda b: (0, 0)),
        ],
        out_specs=pl.BlockSpec((seq, h * GLA_DV), lambda b: (b, 0)),
        out_shape=jax.ShapeDtypeStruct((m, h * GLA_DV), BF16),
        scratch_shapes=[
            pltpu.VMEM((seq, h * GLA_DK), F32),
            pltpu.VMEM((seq, GLA_DK), BF16),
            pltpu.VMEM((seq, GLA_DK), BF16),
            pltpu.VMEM((seq, GLA_DK), BF16),
            pltpu.VMEM((n_pairs, 2, 8, GLA_DK), F32),
        ] + [
            pltpu.VMEM((slots, 2 * GLA_CHUNK, 2 * GLA_CHUNK), BF16),
            pltpu.VMEM((slots, 2, GLA_DK, GLA_DV), F32),
            pltpu.VMEM((slots, 2, GLA_DK, GLA_DK), F32),
            pltpu.VMEM((slots, 2, GLA_DK, GLA_DV), BF16),
        ] * GLA_LANES,
        compiler_params=_cparams("arbitrary"),
        name="gla",
    )(ga, w2p, b_alpha, proj, norm_w)


def _swa_slope(hq):
    return 2.0 ** (-8.0 * (hq + 1) / SWA_Q_HEADS)


SWA_GROUP = SWA_Q_HEADS // SWA_KV_HEADS
SWA_PAIRS = SWA_KV_HEADS // 2


def _swa_head(pair, half, g):
    return (2 * pair + half) * SWA_GROUP + g


def _swa_head_order():
    return [_swa_head(pair, half, g) for pair in range(SWA_PAIRS) for g in range(SWA_GROUP) for half in range(2)]


def _swa_kernel(sink_ref, p_ref, o_ref, bias_ref, lg_ref, acc_ref, es_ref, *, n_blocks):
    w = SWA_WINDOW
    dh = SWA_HEAD_DIM
    gw = SWA_GROUP * w
    nkv = SWA_KV_HEADS

    @pl.when(pl.program_id(0) == 0)
    def _():
        kpos = lax.broadcasted_iota(jnp.int32, (2 * w, w), 0)
        qpos = lax.broadcasted_iota(jnp.int32, (2 * w, w), 1) + w
        dist = qpos - kpos
        band = (dist >= 0) & (dist < w)
        distf = dist.astype(F32)
        for pair in range(SWA_PAIRS):
            for half in range(2):
                for g in range(SWA_GROUP):
                    alibi = -_swa_slope(_swa_head(pair, half, g)) * distf
                    kv = 2 * pair + half
                    bias_ref[kv, :, g * w:(g + 1) * w] = jnp.where(band, alibi, -jnp.inf)
                    bias_ref[nkv + kv, :, g * w:(g + 1) * w] = jnp.where(band & (kpos >= w), alibi, -jnp.inf)

    lane = lax.broadcasted_iota(jnp.int32, (1, 2 * dh), 1)
    in_half = [lane < dh, lane >= dh]
    sub = lax.broadcasted_iota(jnp.int32, (2 * dh, 1), 0)
    sinks = [jnp.concatenate([jnp.full((1, w), sink_ref[_swa_head(kv // 2, kv % 2, g)], F32)
                              for g in range(SWA_GROUP)], axis=1) for kv in range(nkv)]

    def block_rows(n):
        return pl.ds(n * w if isinstance(n, int) else pl.multiple_of(n * w, w), w)

    q_rows = block_rows

    k_off = SWA_Q_HEADS * dh
    v_off = k_off + nkv * dh

    def kv_block(off, n, pair):
        prev = max(n - 1, 0) if isinstance(n, int) else n - 1
        cols = slice(off + pair * 2 * dh, off + (pair + 1) * 2 * dh)
        return jnp.concatenate([p_ref[block_rows(prev), cols], p_ref[block_rows(n), cols]], axis=0)

    def scores(n, par):
        table = nkv if isinstance(n, int) and n == 0 else 0
        for pair in range(SWA_PAIRS):
            qst = jnp.concatenate(
                [p_ref[q_rows(n), (pair * SWA_GROUP + g) * 2 * dh:(pair * SWA_GROUP + g + 1) * 2 * dh]
                 for g in range(SWA_GROUP)], axis=0)
            kp = kv_block(k_off, n, pair) * (dh ** -0.5)
            for half in range(2):
                kv = 2 * pair + half
                kh = jnp.where(in_half[half], kp, jnp.zeros_like(kp))
                lg_ref[par, kv] = _dot_nt(kh, qst) + bias_ref[table + kv]

    def probs(n, par):
        for pair in range(SWA_PAIRS):
            vp = kv_block(v_off, n, pair)
            for half in range(2):
                kv = 2 * pair + half
                logits = lg_ref[par, kv]
                mx = jnp.maximum(jnp.max(logits, axis=0, keepdims=True), sinks[kv])
                p = jnp.exp(logits - mx).astype(BF16)
                vh = jnp.where(in_half[half], vp, jnp.ones_like(vp))
                acc_ref[par, kv] = _dot_tn(vh, p)
                es_ref[par, kv] = jnp.broadcast_to(jnp.exp(sinks[kv] - mx), (8, gw))

    def outputs(n, par):
        for pair in range(SWA_PAIRS):
            out_t = None
            for half in range(2):
                kv = 2 * pair + half
                acc = acc_ref[par, kv]
                other = (1 - half) * dh
                den = acc[other:other + 1, :] + es_ref[par, kv][0:1, :]
                acc = acc * (1.0 / den)
                out_t = acc if out_t is None else jnp.where(sub < dh, out_t, acc)
            for g in range(SWA_GROUP):
                blk = pair * SWA_GROUP + g
                o_ref[q_rows(n), blk * 2 * dh:(blk + 1) * 2 * dh] = out_t[:, g * w:(g + 1) * w].T.astype(BF16)

    _pipeline3_ring(n_blocks, scores, probs, outputs)


def _swa(proj, sinks, batch, seq):
    m = proj.shape[0]
    w = SWA_WINDOW
    qw = SWA_Q_HEADS * SWA_HEAD_DIM
    kern = functools.partial(_swa_kernel, n_blocks=seq // w)
    return pl.pallas_call(
        kern,
        grid=(batch,),
        in_specs=[
            pl.BlockSpec(memory_space=pltpu.SMEM),
            pl.BlockSpec((seq, P_TOTAL - P_SQ), lambda b: (b, P_SQ // (P_TOTAL - P_SQ))),
        ],
        out_specs=pl.BlockSpec((seq, qw), lambda b: (b, 0)),
        out_shape=jax.ShapeDtypeStruct((m, qw), BF16),
        scratch_shapes=[
            pltpu.VMEM((2 * SWA_KV_HEADS, 2 * w, SWA_GROUP * w), F32),
            pltpu.VMEM((2, SWA_KV_HEADS, 2 * w, SWA_GROUP * w), F32),
            pltpu.VMEM((2, SWA_KV_HEADS, 2 * SWA_HEAD_DIM, SWA_GROUP * w), F32),
            pltpu.VMEM((2, SWA_KV_HEADS, 8, SWA_GROUP * w), F32),
        ],
        compiler_params=_cparams("arbitrary"),
        name="swa",
    )(sinks, proj)


def _merge_kernel(x_ref, gl_ref, r_ref, g_ref, s_ref, wb_ref, wo_ref, nw_ref, o_ref):
    d = D_MODEL
    merged = None
    for n, br in enumerate((r_ref, g_ref, s_ref)):
        p = _dot(br[...], wb_ref[n])
        gate = jax.nn.sigmoid(gl_ref[:, n * d:(n + 1) * d].astype(F32))
        merged = gate * p if merged is None else merged + gate * p
    y = _dot(merged.astype(BF16), wo_ref[...])
    o_ref[...] = x_ref[...] + _rms(y, nw_ref[...], NORM_EPS)


def _merge(x2, proj, o_ret, o_gla, o_swa, wb, wo, nw, layer, tm):
    m, d = x2.shape
    const = pl.Buffered(1)
    return pl.pallas_call(
        _merge_kernel,
        grid=(m // tm,),
        in_specs=[
            pl.BlockSpec((tm, d), lambda i: (i, 0)),
            pl.BlockSpec((tm, N_BRANCH * d), lambda i: (i, P_GATE // (N_BRANCH * d))),
            pl.BlockSpec((tm, d), lambda i: (i, 0)),
            pl.BlockSpec((tm, d), lambda i: (i, 0)),
            pl.BlockSpec((tm, d), lambda i: (i, 0)),
            pl.BlockSpec((None, N_BRANCH, d, d), lambda i: (layer, 0, 0, 0), pipeline_mode=const),
            pl.BlockSpec((None, d, d), lambda i: (layer, 0, 0), pipeline_mode=const),
            pl.BlockSpec((1, d), lambda i: (0, 0)),
        ],
        out_specs=pl.BlockSpec((tm, d), lambda i: (i, 0)),
        out_shape=jax.ShapeDtypeStruct((m, d), F32),
        compiler_params=_cparams("arbitrary"),
        name="merge",
    )(x2, proj, o_ret, o_gla, o_swa, wb, wo, nw)


def _ffn_kernel(x_ref, wg_ref, wu_ref, wd_ref, npre_ref, npost_ref, *rest):
    x = x_ref[...]
    h = _rms(x, npre_ref[...], NORM_EPS).astype(BF16)
    gate = _dot(h, wg_ref[...])
    up = _dot(h, wu_ref[...])
    act = (gate * jax.nn.sigmoid(gate) * up).astype(BF16)
    f = _dot(act, wd_ref[...])
    out = x + _rms(f, npost_ref[...], NORM_EPS)
    if len(rest) == 1:
        (o_ref,) = rest
    else:
        nnext_ref, o_ref, hn_ref = rest
        hn_ref[...] = _rms(out, nnext_ref[...], NORM_EPS).astype(BF16)
    o_ref[...] = out


def _ffn(x2, wg, wu, wd, npre, npost, layer, tm, next_gain=None):
    m, d = x2.shape
    ff = wg.shape[2]
    const = pl.Buffered(1)
    row_spec = pl.BlockSpec((tm, d), lambda i: (i, 0))
    vec_spec = pl.BlockSpec((1, d), lambda i: (0, 0))
    in_specs = [
        row_spec,
        pl.BlockSpec((None, d, ff), lambda i: (layer, 0, 0), pipeline_mode=const),
        pl.BlockSpec((None, d, ff), lambda i: (layer, 0, 0), pipeline_mode=const),
        pl.BlockSpec((None, ff, d), lambda i: (layer, 0, 0), pipeline_mode=const),
        vec_spec,
        vec_spec,
    ]
    args = [x2, wg, wu, wd, npre, npost]
    out_specs, out_shape = row_spec, jax.ShapeDtypeStruct((m, d), F32)
    if next_gain is not None:
        in_specs.append(vec_spec)
        args.append(next_gain)
        out_specs = [row_spec, row_spec]
        out_shape = [out_shape, jax.ShapeDtypeStruct((m, d), BF16)]
    return pl.pallas_call(
        _ffn_kernel,
        grid=(m // tm,),
        in_specs=in_specs,
        out_specs=out_specs,
        out_shape=out_shape,
        compiler_params=_cparams("arbitrary"),
        name="ffn",
    )(*args)


def _tile(m, want):
    t = min(want, m)
    assert m % t == 0
    return t


def kernel(x, norm_mix_pre, norm_mix_post, w_in, ret_norm_w, ret_norm_b, gla_w_alpha2, gla_b_alpha,
           gla_norm_w, attn_sinks, w_branch, w_out, norm_ffn_pre, norm_ffn_post, ffn_w_gate, ffn_w_up,
           ffn_w_down):
    batch, seq, d = x.shape
    assert d == D_MODEL and seq % SWA_WINDOW == 0
    m = batch * seq
    x2 = x.reshape(m, d)
    log_g = jnp.log1p(-jnp.exp2(-5.0 - jnp.arange(RET_HEADS, dtype=F32)))
    log_g = jnp.repeat(log_g, LANES)[None, :]

    layers = w_in.shape[0]

    def swa_order(t, axis):
        shape = t.shape
        t = t.reshape(shape[:axis] + (SWA_PAIRS, 2, SWA_GROUP, SWA_HEAD_DIM) + shape[axis + 1:])
        t = jnp.swapaxes(t, axis + 1, axis + 2)
        return t.reshape(shape)

    tail = lax.optimization_barrier(w_in[:, :, _O_SWA:_O_END])
    n_sq = SWA_Q_HEADS * SWA_HEAD_DIM
    n_gate = _O_END - _O_GATE
    w_main = jnp.concatenate(
        [tail[:, :, -n_gate:], w_in[:, :, _O_RET:_O_GA], swa_order(tail[:, :, :n_sq], 2),
         tail[:, :, n_sq:-n_gate]], axis=2).astype(BF16)
    w_ga = jnp.pad(w_in[:, :, _O_GA:_O_SWA], ((0, 0), (0, 0), (0, LANES - GLA_RANK))).astype(BF16)
    w2p = jnp.pad(gla_w_alpha2, ((0, 0), (0, LANES - GLA_RANK), (0, 0))).astype(BF16)
    w_br = jnp.concatenate([w_branch[:, :2], swa_order(w_branch[:, 2], 1)[:, None]], axis=1).astype(BF16)
    w_o = w_out.astype(BF16)
    w_fg, w_fu, w_fd = ffn_w_gate.astype(BF16), ffn_w_up.astype(BF16), ffn_w_down.astype(BF16)
    assert w_main.shape == (layers, d, P_TOTAL)

    h_next = None
    for l in range(DEPTH):
        if h_next is None:
            proj, ga = _inproj(x2, norm_mix_pre[l][None, :], w_main, w_ga, l, _tile(m, 1024), 1792)
        else:
            proj, ga = _inproj_normed(h_next, w_main, w_ga, l, _tile(m, 2048), 1792)
        o_ret = _retention(proj, log_g, ret_norm_w[l][None, :], ret_norm_b[l][None, :], batch, seq)
        o_gla = _gla(proj, ga, w2p[l], gla_b_alpha[l][None, :], gla_norm_w[l][None, :], batch, seq)
        o_swa = _swa(proj, attn_sinks[l], batch, seq)
        x2 = _merge(x2, proj, o_ret, o_gla, o_swa, w_br, w_o, norm_mix_post[l][None, :], l, _tile(m, 512))
        if l + 1 < DEPTH:
            x2, h_next = _ffn(x2, w_fg, w_fu, w_fd, norm_ffn_pre[l][None, :], norm_ffn_post[l][None, :], l,
                              _tile(m, 512), next_gain=norm_mix_pre[l + 1][None, :])
        else:
            x2 = _ffn(x2, w_fg, w_fu, w_fd, norm_ffn_pre[l][None, :], norm_ffn_post[l][None, :], l, _tile(m, 512))
    return x2.reshape(batch, seq, d)
```
